```python
import math
import jax, jax.numpy as jnp
from jax import lax
import numpy as np

D_MODEL = 2048
BATCH = 8
SEQ = 4096
DEPTH = 4

N_MIXERS = 2
N_EVEN = (DEPTH + 1) // 2
N_ODD = DEPTH // 2
RMS_EPS = 1e-6

DA_HEADS = 8
DA_HEAD_DIM = D_MODEL // DA_HEADS // 2
DA_V_DIM = 2 * DA_HEAD_DIM
ROT_DIM = DA_HEAD_DIM // 4
ROPE_THETA = 500000.0
Q_BLOCK = 128

GDN_HEAD_DIM = 128
GDN_K_HEADS = D_MODEL // GDN_HEAD_DIM
GDN_V_HEADS = 2 * GDN_K_HEADS
GDN_KEY_DIM = GDN_K_HEADS * GDN_HEAD_DIM
GDN_VAL_DIM = GDN_V_HEADS * GDN_HEAD_DIM
GDN_CONV_DIM = 2 * GDN_KEY_DIM + GDN_VAL_DIM
GDN_PROJ = GDN_CONV_DIM + GDN_VAL_DIM + 2 * GDN_V_HEADS
GDN_CONV = 4
GDN_CHUNK = 64

D_FF = 5632
N_EXPERTS = 8
TOP_K = 2
EXPERT_FF = 5632

kernel_name = 'hybrid_diffattn_gdn_moe_adaln'


def rmsnorm(x, w):
    xf = x.astype(jnp.float32)
    y = xf * lax.rsqrt(jnp.mean(xf * xf, axis=-1, keepdims=True) + RMS_EPS)
    return (y * w.astype(jnp.float32)).astype(x.dtype)


def rope_tables(positions):
    inv_freq = 1.0 / (ROPE_THETA ** (jnp.arange(0, ROT_DIM, 2, dtype=jnp.float32) / ROT_DIM))
    ang = positions.astype(jnp.float32)[..., None] * inv_freq
    return jnp.cos(ang), jnp.sin(ang)


def apply_partial_rope(x, cos, sin):
    c = cos[:, :, None, None, :].astype(x.dtype)
    s = sin[:, :, None, None, :].astype(x.dtype)
    half = ROT_DIM // 2
    x1 = x[..., :half]
    x2 = x[..., half:ROT_DIM]
    return jnp.concatenate([x1 * c - x2 * s, x2 * c + x1 * s, x[..., ROT_DIM:]], axis=-1)


def diff_attention(h, w_in, lam_vec, subln, w_out, cos, sin, layer_idx):
    B, S, D = h.shape
    q, k, v = jnp.split(h @ w_in, 3, axis=-1)
    q = q.reshape(B, S, DA_HEADS, 2, DA_HEAD_DIM)
    k = k.reshape(B, S, DA_HEADS, 2, DA_HEAD_DIM)
    v = v.reshape(B, S, DA_HEADS, DA_V_DIM)
    q = apply_partial_rope(q, cos, sin) * (DA_HEAD_DIM ** -0.5)
    k = apply_partial_rope(k, cos, sin)
    lambda_init = 0.8 - 0.6 * math.exp(-0.3 * layer_idx)
    lv = lam_vec.astype(jnp.float32)
    lam = jnp.exp(jnp.sum(lv[0] * lv[1])) - jnp.exp(jnp.sum(lv[2] * lv[3])) + lambda_init
    nb = S // Q_BLOCK
    q_blocks = jnp.moveaxis(q.reshape(B, nb, Q_BLOCK, DA_HEADS, 2, DA_HEAD_DIM), 1, 0)
    k_pos = jnp.arange(S)
    neg = jnp.finfo(jnp.float32).min

    def block(args):
        qb, bi = args
        s = jnp.einsum('bqhjd,bkhjd->bhjqk', qb, k).astype(jnp.float32)
        q_pos = bi * Q_BLOCK + jnp.arange(Q_BLOCK)
        s = jnp.where(k_pos[None, :] <= q_pos[:, None], s, neg)
        p = jax.nn.softmax(s, axis=-1)
        a = p[:, :, 0] - lam * p[:, :, 1]
        return jnp.einsum('bhqk,bkhe->bqhe', a.astype(v.dtype), v)

    o = lax.map(block, (q_blocks, jnp.arange(nb)))
    o = jnp.moveaxis(o, 0, 1).reshape(B, S, DA_HEADS, DA_V_DIM)
    o = rmsnorm(o, subln) * (1.0 - lambda_init)
    return o.reshape(B, S, D) @ w_out


def causal_depthwise_conv(x, w):
    K, C = w.shape
    return lax.conv_general_dilated(
        x, w.reshape(K, 1, C).astype(x.dtype), window_strides=(1,), padding=[(K - 1, 0)],
        dimension_numbers=('NWC', 'WIO', 'NWC'), feature_group_count=C)


def l2norm(x):
    xf = x.astype(jnp.float32)
    return xf * lax.rsqrt(jnp.sum(xf * xf, axis=-1, keepdims=True) + 1e-6)


def chunk_gated_delta_rule(q, k, v, g, beta):
    B, S, H, Dk = q.shape
    Dv = v.shape[-1]
    C = GDN_CHUNK
    N = S // C

    def to_chunks(t):
        t = jnp.moveaxis(t.astype(jnp.float32), 2, 1)
        return t.reshape((B, H, N, C) + t.shape[3:])

    q, k, v, g, beta = (to_chunks(t) for t in (q, k, v, g, beta))
    g_cum = jnp.cumsum(g, axis=-1)
    idx = jnp.arange(C)
    tril = idx[:, None] >= idx[None, :]
    strict = idx[:, None] > idx[None, :]
    decay = jnp.exp(jnp.where(tril, g_cum[..., :, None] - g_cum[..., None, :], -jnp.inf))
    k_beta = k * beta[..., None]
    v_beta = v * beta[..., None]
    m = jnp.where(strict, jnp.einsum('bhncd,bhnsd->bhncs', k_beta, k) * decay, 0.0)
    eye = jnp.eye(C, dtype=jnp.float32)
    t_inv = lax.linalg.triangular_solve(eye + m, jnp.broadcast_to(eye, m.shape),
                                        left_side=True, lower=True, unit_diagonal=True)
    u = t_inv @ v_beta
    w = t_inv @ (k_beta * jnp.exp(g_cum)[..., None])
    intra = jnp.einsum('bhncd,bhnsd->bhncs', q, k) * decay
    q_dec = q * jnp.exp(g_cum)[..., None]
    k_dec = k * jnp.exp(g_cum[..., -1:] - g_cum)[..., None]
    g_end = jnp.exp(g_cum[..., -1])
    xs = tuple(jnp.moveaxis(t, 2, 0) for t in (u, w, intra, q_dec, k_dec, g_end))

    def step(state, inp):
        u_n, w_n, a_n, qd_n, kd_n, ge_n = inp
        v_new = u_n - jnp.einsum('bhcd,bhde->bhce', w_n, state)
        out = jnp.einsum('bhcd,bhde->bhce', qd_n, state) + jnp.einsum('bhcs,bhse->bhce', a_n, v_new)
        state = state * ge_n[..., None, None] + jnp.einsum('bhcd,bhce->bhde', kd_n, v_new)
        return state, out

    state0 = jnp.zeros((B, H, Dk, Dv), jnp.float32)
    _, out = lax.scan(step, state0, xs)
    out = jnp.moveaxis(out, 0, 2).reshape(B, H, S, Dv)
    return jnp.moveaxis(out, 1, 2)


def gated_deltanet(h, w_in, conv_w, a_log, dt_bias, norm_w, w_out):
    B, S, _ = h.shape
    proj = h @ w_in
    qkv, z, b_raw, a_raw = jnp.split(
        proj, [GDN_CONV_DIM, GDN_CONV_DIM + GDN_VAL_DIM, GDN_CONV_DIM + GDN_VAL_DIM + GDN_V_HEADS], axis=-1)
    qkv = jax.nn.silu(causal_depthwise_conv(qkv, conv_w))
    q, k, v = jnp.split(qkv, [GDN_KEY_DIM, 2 * GDN_KEY_DIM], axis=-1)
    rep = GDN_V_HEADS // GDN_K_HEADS
    q = l2norm(q.reshape(B, S, GDN_K_HEADS, GDN_HEAD_DIM)) * (GDN_HEAD_DIM ** -0.5)
    k = l2norm(k.reshape(B, S, GDN_K_HEADS, GDN_HEAD_DIM))
    q = jnp.repeat(q, rep, axis=2)
    k = jnp.repeat(k, rep, axis=2)
    v = v.reshape(B, S, GDN_V_HEADS, GDN_HEAD_DIM)
    beta = jax.nn.sigmoid(b_raw.astype(jnp.float32))
    g = -jnp.exp(a_log.astype(jnp.float32)) * jax.nn.softplus(a_raw.astype(jnp.float32) + dt_bias.astype(jnp.float32))
    o = chunk_gated_delta_rule(q, k, v, g, beta)
    zf = z.reshape(B, S, GDN_V_HEADS, GDN_HEAD_DIM).astype(jnp.float32)
    o = rmsnorm(o, norm_w) * jax.nn.silu(zf)
    return o.reshape(B, S, GDN_VAL_DIM).astype(h.dtype) @ w_out


def swiglu(h, wg, wu, wd):
    return (jax.nn.silu(h @ wg) * (h @ wu)) @ wd


def moe_swiglu(h, router, wg, wu, wd):
    logits = (h @ router).astype(jnp.float32)
    top_v, top_i = lax.top_k(logits, TOP_K)
    top_w = jax.nn.softmax(top_v, axis=-1)
    gates = jnp.sum(jax.nn.one_hot(top_i, N_EXPERTS, dtype=jnp.float32) * top_w[..., None], axis=-2)
    y = jnp.zeros_like(h)
    for e in range(N_EXPERTS):
        y = y + gates[..., e:e + 1].astype(h.dtype) * swiglu(h, wg[e], wu[e], wd[e])
    return y


def setup_inputs(seed: int = 0) -> dict:
    key = jax.random.key(seed)
    ks = jax.random.split(key, 26)
    f32 = jnp.float32
    D = D_MODEL

    def normal(k, shape, scale):
        return jax.random.normal(k, shape, f32) * scale

    x = normal(ks[0], (BATCH, SEQ, D), 1.0)
    c = normal(ks[1], (BATCH, D), 1.0)
    offsets = jax.random.randint(ks[2], (BATCH, 1), 0, 1024, dtype=jnp.int32)
    positions = offsets + jnp.arange(SEQ, dtype=jnp.int32)[None, :]
    ada_w = normal(ks[3], (DEPTH, D, 6 * D), 0.5 * D ** -0.5)
    ada_b = normal(ks[4], (DEPTH, 6 * D), 0.02)
    norm1_w = 1.0 + normal(ks[5], (DEPTH, D), 0.02)
    norm2_w = 1.0 + normal(ks[6], (DEPTH, D), 0.02)
    attn_w_in = normal(ks[7], (N_EVEN, D, 3 * D), D ** -0.5)
    attn_lambda = normal(ks[8], (N_EVEN, 4, DA_HEAD_DIM), 0.1)
    attn_subln = 1.0 + normal(ks[9], (N_EVEN, DA_V_DIM), 0.02)
    attn_w_out = normal(ks[10], (N_EVEN, D, D), D ** -0.5)
    gdn_w_in = normal(ks[11], (N_ODD, D, GDN_PROJ), D ** -0.5)
    gdn_conv_w = normal(ks[12], (N_ODD, GDN_CONV, GDN_CONV_DIM), GDN_CONV ** -0.5)
    gdn_a_log = jnp.log(jax.random.uniform(ks[13], (N_ODD, GDN_V_HEADS), f32, 1.0, 16.0))
    dt = jnp.exp(jax.random.uniform(ks[14], (N_ODD, GDN_V_HEADS), f32, math.log(1e-3), math.log(1e-1)))
    gdn_dt_bias = dt + jnp.log(-jnp.expm1(-dt))
    gdn_norm_w = 1.0 + normal(ks[15], (N_ODD, GDN_HEAD_DIM), 0.02)
    gdn_w_out = normal(ks[16], (N_ODD, GDN_VAL_DIM, D), GDN_VAL_DIM ** -0.5)
    ffn_w_gate = normal(ks[17], (N_EVEN, D, D_FF), D ** -0.5)
    ffn_w_up = normal(ks[18], (N_EVEN, D, D_FF), D ** -0.5)
    ffn_w_down = normal(ks[19], (N_EVEN, D_FF, D), D_FF ** -0.5)
    moe_router = normal(ks[20], (N_ODD, D, N_EXPERTS), D ** -0.5)
    moe_w_gate = normal(ks[21], (N_ODD, N_EXPERTS, D, EXPERT_FF), D ** -0.5)
    moe_w_up = normal(ks[22], (N_ODD, N_EXPERTS, D, EXPERT_FF), D ** -0.5)
    moe_w_down = normal(ks[23], (N_ODD, N_EXPERTS, EXPERT_FF, D), EXPERT_FF ** -0.5)
    final_norm_w = 1.0 + normal(ks[24], (D,), 0.02)
    return {'x': x, 'c': c, 'positions': positions, 'ada_w': ada_w, 'ada_b': ada_b,
            'norm1_w': norm1_w, 'norm2_w': norm2_w, 'attn_w_in': attn_w_in,
            'attn_lambda': attn_lambda, 'attn_subln': attn_subln, 'attn_w_out': attn_w_out,
            'gdn_w_in': gdn_w_in, 'gdn_conv_w': gdn_conv_w, 'gdn_a_log': gdn_a_log,
            'gdn_dt_bias': gdn_dt_bias, 'gdn_norm_w': gdn_norm_w, 'gdn_w_out': gdn_w_out,
            'ffn_w_gate': ffn_w_gate, 'ffn_w_up': ffn_w_up, 'ffn_w_down': ffn_w_down,
            'moe_router': moe_router, 'moe_w_gate': moe_w_gate, 'moe_w_up': moe_w_up,
            'moe_w_down': moe_w_down, 'final_norm_w': final_norm_w}


def reference(x, c, positions, ada_w, ada_b, norm1_w, norm2_w, attn_w_in, attn_lambda,
              attn_subln, attn_w_out, gdn_w_in, gdn_conv_w, gdn_a_log, gdn_dt_bias,
              gdn_norm_w, gdn_w_out, ffn_w_gate, ffn_w_up, ffn_w_down, moe_router,
              moe_w_gate, moe_w_up, moe_w_down, final_norm_w):
    cos, sin = rope_tables(positions)
    c_act = jax.nn.silu(c)
    h = x
    for i in range(DEPTH):
        j = i // N_MIXERS
        mod = c_act @ ada_w[i] + ada_b[i]
        sh1, sc1, g1, sh2, sc2, g2 = [m[:, None, :] for m in jnp.split(mod, 6, axis=-1)]
        u = rmsnorm(h, norm1_w[i]) * (1.0 + sc1) + sh1
        if i % N_MIXERS == 0:
            mix = diff_attention(u, attn_w_in[j], attn_lambda[j], attn_subln[j], attn_w_out[j], cos, sin, i)
        else:
            mix = gated_deltanet(u, gdn_w_in[j], gdn_conv_w[j], gdn_a_log[j], gdn_dt_bias[j],
                                 gdn_norm_w[j], gdn_w_out[j])
        h = h + g1 * mix
        u = rmsnorm(h, norm2_w[i]) * (1.0 + sc2) + sh2
        if i % 2 == 0:
            ff = swiglu(u, ffn_w_gate[j], ffn_w_up[j], ffn_w_down[j])
        else:
            ff = moe_swiglu(u, moe_router[j], moe_w_gate[j], moe_w_up[j], moe_w_down[j])
        h = h + g2 * ff
    return rmsnorm(h, final_norm_w)
```

```python
import functools
import math

import jax
import jax.numpy as jnp
from jax import lax
from jax.experimental import pallas as pl
from jax.experimental.pallas import tpu as pltpu

F32 = jnp.float32
BF16 = jnp.bfloat16
I32 = jnp.int32

LANE = 128
V7X_VMEM_BYTES = 64 * 1024 * 1024
VMEM_LIMIT = V7X_VMEM_BYTES * 3 // 4

RMS_EPS = 1e-6
DA_HEADS = 8
ROPE_THETA = 500000.0
GDN_HEAD_DIM = 128
GDN_CHUNK = 64
L2_EPS = 1e-6
TOP_K = 2
NEG = -1e30


def _cparams(*sem):
    return pltpu.CompilerParams(dimension_semantics=sem, vmem_limit_bytes=VMEM_LIMIT)


def _silu(x):
    return x / (1.0 + jnp.exp(-x))


def _adaln_kernel(c_ref, w_ref, b_ref, o_ref):
    ca = _silu(c_ref[...]).astype(BF16)
    o_ref[0] = jnp.dot(ca, w_ref[0].astype(BF16), preferred_element_type=F32) + b_ref[0]


def _adaln(c, ada_w, ada_b):
    depth, d, n = ada_w.shape
    b = c.shape[0]
    tn = 1024
    return pl.pallas_call(
        _adaln_kernel,
        grid=(depth, n // tn),
        in_specs=[pl.BlockSpec((b, d), lambda l, j: (0, 0)),
                  pl.BlockSpec((1, d, tn), lambda l, j: (l, 0, j)),
                  pl.BlockSpec((1, 1, tn), lambda l, j: (l, 0, j))],
        out_specs=pl.BlockSpec((1, b, tn), lambda l, j: (l, 0, j)),
        out_shape=jax.ShapeDtypeStruct((depth, b, n), F32),
        compiler_params=_cparams("parallel", "parallel"), name="adaln",
    )(c, ada_w, ada_b.reshape(depth, 1, n))


def _norm_kernel(*refs, has_delta, has_mod, n_experts):
    refs = list(refs)
    h_ref = refs.pop(0)
    if has_delta:
        d_ref, g_ref = refs.pop(0), refs.pop(0)
    w_ref = refs.pop(0)
    if has_mod:
        sc_ref, sh_ref = refs.pop(0), refs.pop(0)
    if n_experts:
        r_ref = refs.pop(0)
    if has_delta:
        hn_ref = refs.pop(0)
    u_ref = refs.pop(0)
    if n_experts:
        gates_ref, sel_ref = refs.pop(0), refs.pop(0)

    x = h_ref[...]
    if has_delta:
        x = x + g_ref[0] * d_ref[...]
        hn_ref[...] = x
    y = x * lax.rsqrt(jnp.mean(x * x, axis=-1, keepdims=True) + RMS_EPS) * w_ref[...]
    if has_mod:
        y = y * (1.0 + sc_ref[0]) + sh_ref[0]
    u_ref[...] = y.astype(u_ref.dtype)
    if n_experts:
        logits = jnp.dot(y, r_ref[...], preferred_element_type=F32, precision=lax.Precision.HIGHEST)
        lane = lax.broadcasted_iota(I32, logits.shape, 1)
        logits = jnp.where(lane < n_experts, logits, -jnp.inf)
        m1 = jnp.max(logits, axis=1, keepdims=True)
        i1 = jnp.min(jnp.where(logits == m1, lane, LANE), axis=1, keepdims=True)
        oh1 = lane == i1
        rest = jnp.where(oh1, -jnp.inf, logits)
        m2 = jnp.max(rest, axis=1, keepdims=True)
        i2 = jnp.min(jnp.where(rest == m2, lane, LANE), axis=1, keepdims=True)
        oh2 = lane == i2
        e = jnp.exp(m2 - m1)
        gates_ref[...] = jnp.where(oh1, 1.0 / (1.0 + e), 0.0) + jnp.where(oh2, e / (1.0 + e), 0.0)
        sel_ref[...] = jnp.where(oh1 | oh2, 1.0, 0.0)


def _norm(h, w, *, seq, delta=None, gate=None, sc=None, sh=None, router=None, n_experts=0,
          out_dtype=BF16):
    t, d = h.shape
    tm = 256
    per_seq = seq // tm
    row = pl.BlockSpec((tm, d), lambda i: (i, 0))
    vec = pl.BlockSpec((1, d), lambda i: (0, 0))
    bvec = pl.BlockSpec((1, 1, d), lambda i: (i // per_seq, 0, 0))
    args, in_specs = [h], [row]
    if delta is not None:
        args += [delta, gate]
        in_specs += [row, bvec]
    args.append(w.reshape(1, d))
    in_specs.append(vec)
    if sc is not None:
        args += [sc, sh]
        in_specs += [bvec, bvec]
    if router is not None:
        args.append(router)
        in_specs.append(pl.BlockSpec((d, LANE), lambda i: (0, 0)))
    out_shape, out_specs = [], []
    if delta is not None:
        out_shape.append(jax.ShapeDtypeStruct((t, d), F32))
        out_specs.append(row)
    out_shape.append(jax.ShapeDtypeStruct((t, d), out_dtype))
    out_specs.append(row)
    if router is not None:
        out_shape += [jax.ShapeDtypeStruct((t, LANE), F32)] * 2
        out_specs += [pl.BlockSpec((tm, LANE), lambda i: (i, 0))] * 2
    kern = functools.partial(_norm_kernel, has_delta=delta is not None, has_mod=sc is not None,
                             n_experts=n_experts)
    return pl.pallas_call(kern, grid=(t // tm,), in_specs=in_specs, out_specs=out_specs,
                          out_shape=out_shape, compiler_params=_cparams("parallel"), name="norm")(*args)


def _rope_group(x, c, s1, s2, half):
    return x * c + pltpu.roll(x, LANE - half, axis=1) * s1 + pltpu.roll(x, half, axis=1) * s2


def _mm_kernel(*refs, mode, n_q, n_qk, q_scale, rope_half):
    if mode == "rope":
        a_ref, w_ref, c_ref, s1_ref, s2_ref, o_ref = refs
    elif mode == "resid":
        a_ref, w_ref, h_ref, g_ref, o_ref = refs
    else:
        a_ref, w_ref, o_ref = refs
    acc = jnp.dot(a_ref[...], w_ref[...], preferred_element_type=F32)
    if mode == "resid":
        o_ref[...] = h_ref[...] + g_ref[0] * acc
    elif mode == "rope":
        j = pl.program_id(1)
        tn = acc.shape[1]

        @pl.when(j < n_qk)
        def _():
            c, s1, s2 = c_ref[...], s1_ref[...], s2_ref[...]
            scale = jnp.where(j < n_q, q_scale, 1.0)
            for g in range(tn // LANE):
                sl = slice(g * LANE, (g + 1) * LANE)
                o_ref[:, sl] = (_rope_group(acc[:, sl], c, s1, s2, rope_half) * scale).astype(o_ref.dtype)

        @pl.when(j >= n_qk)
        def _():
            o_ref[...] = acc.astype(o_ref.dtype)
    else:
        o_ref[...] = acc.astype(o_ref.dtype)


def _matmul(a, w, *, out_dtype=BF16, mode="plain", seq=None, h=None, gate=None, rope=None,
            rope_cols=(0, 0), rope_half=0, q_scale=1.0, tm=1024, tn=512):
    m, k = a.shape
    n = w.shape[1]
    tm, tn = min(tm, m, seq or m), min(tn, n)
    in_specs = [pl.BlockSpec((tm, k), lambda i, j: (i, 0)), pl.BlockSpec((k, tn), lambda i, j: (0, j))]
    args = [a, w]
    if mode == "rope":
        tab = pl.BlockSpec((tm, LANE), lambda i, j: (i, 0))
        in_specs += [tab, tab, tab]
        args += list(rope)
    elif mode == "resid":
        per_seq = seq // tm
        in_specs += [pl.BlockSpec((tm, tn), lambda i, j: (i, j)),
                     pl.BlockSpec((1, 1, tn), lambda i, j: (i // per_seq, 0, j))]
        args += [h, gate]
    kern = functools.partial(_mm_kernel, mode=mode, n_q=rope_cols[0] // tn, n_qk=rope_cols[1] // tn,
                             q_scale=q_scale, rope_half=rope_half)
    return pl.pallas_call(
        kern, grid=(m // tm, n // tn), in_specs=in_specs,
        out_specs=pl.BlockSpec((tm, tn), lambda i, j: (i, j)),
        out_shape=jax.ShapeDtypeStruct((m, n), out_dtype),
        compiler_params=_cparams("parallel", "parallel"), name="proj_" + mode,
    )(*args)


def _attn_kernel(lam_ref, q_ref, k_ref, v_ref, sub_ref, o_ref, m_ref, l_ref, acc_ref, *,
                 blk, dh, lambda_init):
    qi, ki = pl.program_id(2), pl.program_id(3)

    @pl.when(ki == 0)
    def _():
        m_ref[...] = jnp.full(m_ref.shape, NEG, F32)
        l_ref[...] = jnp.zeros(l_ref.shape, F32)
        acc_ref[...] = jnp.zeros(acc_ref.shape, F32)

    @pl.when(ki <= qi)
    def _():
        q, k, v = q_ref[...], k_ref[...], v_ref[...]
        row = lax.broadcasted_iota(I32, (blk, blk), 0) + qi * blk
        col = lax.broadcasted_iota(I32, (blk, blk), 1) + ki * blk
        causal = col <= row
        for j in range(2):
            sl = slice(j * dh, (j + 1) * dh)
            s = lax.dot_general(q[:, sl], k[:, sl], (((1,), (1,)), ((), ())), preferred_element_type=F32)
            s = jnp.where(causal, s, NEG)
            m_prev = m_ref[j]
            m_new = jnp.maximum(m_prev, jnp.max(s, axis=1, keepdims=True))
            alpha = jnp.exp(m_prev - m_new)
            p = jnp.exp(s - m_new)
            l_ref[j] = alpha * l_ref[j] + jnp.sum(p, axis=1, keepdims=True)
            acc_ref[j] = alpha * acc_ref[j] + jnp.dot(p.astype(BF16), v, preferred_element_type=F32)
            m_ref[j] = m_new

    @pl.when(ki == pl.num_programs(3) - 1)
    def _():
        lv = lam_ref[...]
        lam = (jnp.exp(jnp.sum(lv[0:1] * lv[1:2], axis=1, keepdims=True))
               - jnp.exp(jnp.sum(lv[2:3] * lv[3:4], axis=1, keepdims=True)) + lambda_init)
        o = acc_ref[0] / l_ref[0] - lam * (acc_ref[1] / l_ref[1])
        o = o * lax.rsqrt(jnp.mean(o * o, axis=-1, keepdims=True) + RMS_EPS) * sub_ref[...]
        o_ref[...] = (o * (1.0 - lambda_init)).astype(o_ref.dtype)


def _diff_attention(qkv, lam_vec, subln, *, batch, seq, layer_idx):
    t, d3 = qkv.shape
    d = d3 // 3
    dv = d // DA_HEADS
    dh = dv // 2
    blk = 512
    nb = seq // blk
    lambda_init = 0.8 - 0.6 * math.exp(-0.3 * layer_idx)
    kern = functools.partial(_attn_kernel, blk=blk, dh=dh, lambda_init=lambda_init)
    return pl.pallas_call(
        kern, grid=(batch, DA_HEADS, nb, nb),
        in_specs=[pl.BlockSpec((4, dh), lambda b, h, qi, ki: (0, 0)),
                  pl.BlockSpec((blk, dv), lambda b, h, qi, ki: (b * nb + qi, h)),
                  pl.BlockSpec((blk, dv), lambda b, h, qi, ki: (b * nb + jnp.minimum(ki, qi), DA_HEADS + h)),
                  pl.BlockSpec((blk, dv), lambda b, h, qi, ki: (b * nb + jnp.minimum(ki, qi), 2 * DA_HEADS + h)),
                  pl.BlockSpec((1, dv), lambda b, h, qi, ki: (0, 0))],
        out_specs=pl.BlockSpec((blk, dv), lambda b, h, qi, ki: (b * nb + qi, h)),
        out_shape=jax.ShapeDtypeStruct((t, d), BF16),
        scratch_shapes=[pltpu.VMEM((2, blk, 1), F32), pltpu.VMEM((2, blk, 1), F32),
                        pltpu.VMEM((2, blk, dv), F32)],
        compiler_params=_cparams("parallel", "parallel", "parallel", "arbitrary"), name="diff_attn",
    )(lam_vec, qkv, qkv, qkv, subln.reshape(1, dv))


def _conv_kernel(x_ref, halo_ref, w_ref, o_ref, *, mode, per_seq, kw):
    i = pl.program_id(0)
    x = x_ref[...].astype(F32)
    halo = jnp.where(i % per_seq == 0, 0.0, halo_ref[8:, :].astype(F32))
    w = w_ref[...]
    head = jnp.concatenate([halo, x[:8]], axis=0)
    y = x * w[kw - 1:kw]
    y_head = head * w[kw - 1:kw]
    for s in range(1, kw):
        y = y + pltpu.roll(x, s, axis=0) * w[kw - 1 - s:kw - s]
        y_head = y_head + pltpu.roll(head, s, axis=0) * w[kw - 1 - s:kw - s]
    y = jnp.concatenate([y_head[8:], y[8:]], axis=0)
    y = _silu(y)
    if mode == "v":
        o_ref[...] = y.astype(o_ref.dtype)
    else:
        scale = GDN_HEAD_DIM ** -0.5 if mode == "q" else 1.0
        for g in range(y.shape[1] // GDN_HEAD_DIM):
            sl = slice(g * GDN_HEAD_DIM, (g + 1) * GDN_HEAD_DIM)
            yg = y[:, sl]
            yn = yg * lax.rsqrt(jnp.sum(yg * yg, axis=-1, keepdims=True) + L2_EPS)
            o_ref[:, sl] = (yn * scale).astype(o_ref.dtype)


def _conv_silu(proj, conv_w, *, col0, width, mode, seq):
    t = proj.shape[0]
    kw = conv_w.shape[0]
    tm, tc = 512, 512
    cb = col0 // tc
    per_seq = seq // tm
    kern = functools.partial(_conv_kernel, mode=mode, per_seq=per_seq, kw=kw)
    return pl.pallas_call(
        kern, grid=(t // tm, width // tc),
        in_specs=[pl.BlockSpec((tm, tc), lambda i, j: (i, cb + j)),
                  pl.BlockSpec((16, tc), lambda i, j: (jnp.maximum(i * (tm // 16) - 1, 0), cb + j)),
                  pl.BlockSpec((kw, tc), lambda i, j: (0, cb + j))],
        out_specs=pl.BlockSpec((tm, tc), lambda i, j: (i, j)),
        out_shape=jax.ShapeDtypeStruct((t, width), BF16),
        compiler_params=_cparams("parallel", "parallel"), name="conv_" + mode,
    )(proj, proj, conv_w)


def _split3(x):
    hi = x.astype(BF16)
    r = x - hi.astype(F32)
    mid = r.astype(BF16)
    lo = (r - mid.astype(F32)).astype(BF16)
    return hi, mid, lo


def _gates_kernel(ba_ref, alog_ref, dt_ref, beta_ref, gc_ref):
    x = ba_ref[...]
    tm = x.shape[0]
    b_raw, a_raw = x[:, :LANE], x[:, LANE:]
    beta_ref[...] = 1.0 / (1.0 + jnp.exp(-b_raw))
    z = a_raw + dt_ref[...]
    softplus = jnp.maximum(z, 0.0) + jnp.log(1.0 + jnp.exp(-jnp.abs(z)))
    g = -jnp.exp(alog_ref[...]) * softplus
    row = lax.broadcasted_iota(I32, (tm, tm), 0)
    col = lax.broadcasted_iota(I32, (tm, tm), 1)
    tri = jnp.where((row >= col) & (row // GDN_CHUNK == col // GDN_CHUNK), 1.0, 0.0).astype(BF16)
    gc_ref[...] = sum(jnp.dot(tri, part, preferred_element_type=F32) for part in _split3(g))


def _gdn_gates(ba, a_log, dt_bias):
    t = ba.shape[0]
    hv = a_log.shape[0]
    tm = 256
    pad = lambda v: jnp.pad(v.astype(F32), (0, LANE - hv)).reshape(1, LANE)
    blk = pl.BlockSpec((tm, LANE), lambda i: (i, 0))
    vec = pl.BlockSpec((1, LANE), lambda i: (0, 0))
    return pl.pallas_call(
        _gates_kernel, grid=(t // tm,),
        in_specs=[pl.BlockSpec((tm, 2 * LANE), lambda i: (i, 0)), vec, vec],
        out_specs=[blk, blk],
        out_shape=[jax.ShapeDtypeStruct((t, LANE), F32)] * 2,
        compiler_params=_cparams("parallel"), name="gdn_gates",
    )(ba, pad(a_log), pad(dt_bias))


def _bdot(a, b):
    return jnp.dot(a.astype(BF16), b.astype(BF16), preferred_element_type=F32)


def _delta_kernel(q_ref, k_ref, v_ref, z_ref, beta_ref, gc_ref, gct_ref, nw_ref, o_ref, state_ref, *,
                  heads, rows):
    hg, blk = pl.program_id(1), pl.program_id(2)
    dk = GDN_HEAD_DIM
    c = GDN_CHUNK
    n_chunks = rows // c

    @pl.when(blk == 0)
    def _():
        state_ref[...] = jnp.zeros(state_ref.shape, F32)

    row = lax.broadcasted_iota(I32, (rows, rows), 0)
    col = lax.broadcasted_iota(I32, (rows, rows), 1)
    same16 = (row >> 4) == (col >> 4)
    same32 = (row >> 5) == (col >> 5)
    same64 = (row // c) == (col // c)
    lower = row >= col
    eye = jnp.where(row == col, 1.0, 0.0)
    lane = lax.broadcasted_iota(I32, (rows, LANE), 1)
    contract_last = (((1,), (1,)), ((), ()))
    contract_first = (((0,), (0,)), ((), ()))

    for g in range(heads):
        hv = hg * heads + g
        kq = slice((g // 2) * dk, (g // 2 + 1) * dk)
        vs = slice(g * dk, (g + 1) * dk)
        qh, kh = q_ref[:, kq], k_ref[:, kq]
        qf, kf, vf = qh.astype(F32), kh.astype(F32), v_ref[:, vs].astype(F32)
        pick = lane == hv
        gc_col = jnp.sum(jnp.where(pick, gc_ref[...], 0.0), axis=1, keepdims=True)
        beta_col = jnp.sum(jnp.where(pick, beta_ref[...], 0.0), axis=1, keepdims=True)
        gc_row = gct_ref[g]
        decay = jnp.exp(jnp.where(lower & same64, gc_col - gc_row, NEG))
        k_beta = kf * beta_col
        kk = lax.dot_general(k_beta.astype(BF16), kh, contract_last, preferred_element_type=F32)
        a = jnp.where(row > col, kk * decay, 0.0)

        d1 = jnp.where(same16, a, 0.0)
        d2 = _bdot(d1, d1)
        d4 = _bdot(d2, d2)
        d8 = _bdot(d4, d4)
        x = eye - d1
        x = x + _bdot(x, d2)
        x = x + _bdot(x, d4)
        x = x + _bdot(x, d8)
        b32 = jnp.where(same32 & jnp.logical_not(same16), a, 0.0)
        x = x - _bdot(x, _bdot(b32, x))
        b64 = jnp.where(jnp.logical_not(same32), a, 0.0)
        eg = jnp.exp(gc_col)
        y = jnp.concatenate([vf * beta_col, k_beta * eg], axis=1)
        y1 = _bdot(x, y)
        uw = y1 - _bdot(x, _bdot(b64, y1))
        u, w = uw[:, :dk], uw[:, dk:]

        qk = lax.dot_general(qh, kh, contract_last, preferred_element_type=F32)
        intra = qk * decay
        q_dec = qf * eg
        last = (row // c) * c + (c - 1)
        gl_col = jnp.sum(jnp.where(col == last, gc_row, 0.0), axis=1, keepdims=True)
        k_dec = kf * jnp.exp(gl_col - gc_col)

        s = state_ref[g]
        v_new, o_state = [], []
        for ci in range(n_chunks):
            rs = slice(ci * c, (ci + 1) * c)
            sb = s.astype(BF16)
            vn = u[rs] - jnp.dot(w[rs].astype(BF16), sb, preferred_element_type=F32)
            o_state.append(jnp.dot(q_dec[rs].astype(BF16), sb, preferred_element_type=F32))
            g_end = jnp.exp(gl_col[ci * c:ci * c + 1])
            s = s * g_end + lax.dot_general(k_dec[rs].astype(BF16), vn.astype(BF16), contract_first,
                                            preferred_element_type=F32)
            v_new.append(vn)
        state_ref[g] = s
        o = jnp.concatenate(o_state, axis=0) + _bdot(intra, jnp.concatenate(v_new, axis=0))
        o = o * lax.rsqrt(jnp.mean(o * o, axis=-1, keepdims=True) + RMS_EPS) * nw_ref[...]
        o_ref[:, vs] = (o * _silu(z_ref[:, vs].astype(F32))).astype(o_ref.dtype)


def _delta_rule(q, k, v, proj, z_col0, beta, gc, gct, norm_w, *, batch, seq):
    t, vdim = v.shape
    dk = GDN_HEAD_DIM
    hv = vdim // dk
    heads = 4
    rows = 256
    nblk = seq // rows
    qw, vw = heads // 2 * dk, heads * dk
    zb = z_col0 // vw
    kern = functools.partial(_delta_kernel, heads=heads, rows=rows)
    tok = lambda b, hg, i: (b * nblk + i, 0)
    return pl.pallas_call(
        kern, grid=(batch, hv // heads, nblk),
        in_specs=[pl.BlockSpec((rows, qw), lambda b, hg, i: (b * nblk + i, hg)),
                  pl.BlockSpec((rows, qw), lambda b, hg, i: (b * nblk + i, hg)),
                  pl.BlockSpec((rows, vw), lambda b, hg, i: (b * nblk + i, hg)),
                  pl.BlockSpec((rows, vw), lambda b, hg, i: (b * nblk + i, zb + hg)),
                  pl.BlockSpec((rows, LANE), tok),
                  pl.BlockSpec((rows, LANE), tok),
                  pl.BlockSpec((heads, 1, rows), lambda b, hg, i: (hg, 0, b * nblk + i)),
                  pl.BlockSpec((1, dk), lambda b, hg, i: (0, 0))],
        out_specs=pl.BlockSpec((rows, vw), lambda b, hg, i: (b * nblk + i, hg)),
        out_shape=jax.ShapeDtypeStruct((t, vdim), BF16),
        scratch_shapes=[pltpu.VMEM((heads, dk, dk), F32)],
        compiler_params=_cparams("parallel", "parallel", "arbitrary"), name="delta_rule",
    )(q, k, v, proj, beta, gc, gct, norm_w.reshape(1, dk))


def _ffn_kernel(te_ref, *refs, mode):
    del te_ref
    if mode == "resid":
        x_ref, wg_ref, wu_ref, wd_ref, h_ref, g_ref, o_ref, acc_ref = refs
    else:
        x_ref, wg_ref, wu_ref, wd_ref, rs_ref, o_ref, acc_ref = refs
    f = pl.program_id(1)

    @pl.when(f == 0)
    def _():
        acc_ref[...] = jnp.zeros(acc_ref.shape, F32)

    x = x_ref[...]
    gate = jnp.dot(x, wg_ref[0], preferred_element_type=F32)
    up = jnp.dot(x, wu_ref[0], preferred_element_type=F32)
    acc_ref[...] += jnp.dot((_silu(gate) * up).astype(BF16), wd_ref[0], preferred_element_type=F32)

    @pl.when(f == pl.num_programs(1) - 1)
    def _():
        if mode == "resid":
            o_ref[...] = h_ref[...] + g_ref[0] * acc_ref[...]
        else:
            o_ref[...] = (acc_ref[...] * rs_ref[...]).astype(o_ref.dtype)


def _ffn(x, wg, wu, wd, tile_expert, *, tm, mode, seq=None, h=None, gate=None, row_scale=None):
    m, d = x.shape
    ff = wg.shape[2]
    tf = 512
    xs = pl.BlockSpec((tm, d), lambda i, f, te: (i, 0))
    in_specs = [xs,
                pl.BlockSpec((1, d, tf), lambda i, f, te: (te[i], 0, f)),
                pl.BlockSpec((1, d, tf), lambda i, f, te: (te[i], 0, f)),
                pl.BlockSpec((1, tf, d), lambda i, f, te: (te[i], f, 0))]
    args = [x, wg, wu, wd]
    if mode == "resid":
        per_seq = seq // tm
        in_specs += [xs, pl.BlockSpec((1, 1, d), lambda i, f, te: (i // per_seq, 0, 0))]
        args += [h, gate]
        out_dtype = F32
    else:
        in_specs.append(pl.BlockSpec((tm, 1), lambda i, f, te: (i, 0)))
        args.append(row_scale)
        out_dtype = BF16
    return pl.pallas_call(
        functools.partial(_ffn_kernel, mode=mode),
        grid_spec=pltpu.PrefetchScalarGridSpec(
            num_scalar_prefetch=1, grid=(m // tm, ff // tf), in_specs=in_specs, out_specs=xs,
            scratch_shapes=[pltpu.VMEM((tm, d), F32)]),
        out_shape=jax.ShapeDtypeStruct((m, d), out_dtype),
        compiler_params=_cparams("parallel", "arbitrary"), name="swiglu_" + mode,
    )(tile_expert, *args)


def _gather_kernel(idx_ref, src_ref, o_ref, sem, *, tm):
    def issue(r, carry):
        pltpu.make_async_copy(src_ref.at[idx_ref[r]], o_ref.at[r], sem).start()
        return carry
    lax.fori_loop(0, tm, issue, 0)
    pltpu.make_async_copy(src_ref.at[pl.ds(0, tm)], o_ref, sem).wait()


def _gather_rows(src, idx, *, tm):
    _, s, l = src.shape
    p = idx.shape[0]
    return pl.pallas_call(
        functools.partial(_gather_kernel, tm=tm), grid=(p // tm,),
        in_specs=[pl.BlockSpec((tm,), lambda i: (i,), memory_space=pltpu.SMEM),
                  pl.BlockSpec(memory_space=pl.ANY)],
        out_specs=pl.BlockSpec((tm, s, l), lambda i: (i, 0, 0)),
        out_shape=jax.ShapeDtypeStruct((p, s, l), src.dtype),
        scratch_shapes=[pltpu.SemaphoreType.DMA(())],
        compiler_params=_cparams("arbitrary"), name="gather_rows",
    )(idx, src)


def _combine_kernel(ia_ref, ib_ref, src_ref, o_ref, a_buf, b_buf, sem_a, sem_b, *, tm):
    def issue(r, carry):
        pltpu.make_async_copy(src_ref.at[ia_ref[r]], a_buf.at[r], sem_a).start()
        pltpu.make_async_copy(src_ref.at[ib_ref[r]], b_buf.at[r], sem_b).start()
        return carry
    lax.fori_loop(0, tm, issue, 0)
    pltpu.make_async_copy(src_ref.at[pl.ds(0, tm)], a_buf, sem_a).wait()
    pltpu.make_async_copy(src_ref.at[pl.ds(0, tm)], b_buf, sem_b).wait()
    o_ref[...] = a_buf[...].astype(F32) + b_buf[...].astype(F32)


def _combine_rows(src, idx_a, idx_b, *, tm):
    _, s, l = src.shape
    t = idx_a.shape[0]
    ispec = pl.BlockSpec((tm,), lambda i: (i,), memory_space=pltpu.SMEM)
    return pl.pallas_call(
        functools.partial(_combine_kernel, tm=tm), grid=(t // tm,),
        in_specs=[ispec, ispec, pl.BlockSpec(memory_space=pl.ANY)],
        out_specs=pl.BlockSpec((tm, s, l), lambda i: (i, 0, 0)),
        out_shape=jax.ShapeDtypeStruct((t, s, l), F32),
        scratch_shapes=[pltpu.VMEM((tm, s, l), src.dtype), pltpu.VMEM((tm, s, l), src.dtype),
                        pltpu.SemaphoreType.DMA(()), pltpu.SemaphoreType.DMA(())],
        compiler_params=_cparams("arbitrary"), name="combine_rows",
    )(idx_a, idx_b, src)


def _dispatch_plan(sel, gates, n_experts, tm):
    t = sel.shape[0]
    seli = sel[:, :n_experts].astype(I32)
    counts = jnp.sum(seli, axis=0)
    padded = (counts + tm - 1) // tm * tm
    ends = jnp.cumsum(padded)
    starts = ends - padded
    rank = jnp.cumsum(seli, axis=0) - seli
    slot = starts[None, :] + rank
    n_slots = t * TOP_K + n_experts * tm
    flat_slot = jnp.where(seli > 0, slot, n_slots).reshape(-1)
    tok = jnp.broadcast_to(jnp.arange(t, dtype=I32)[:, None], slot.shape).reshape(-1)
    token_of_slot = jnp.zeros((n_slots,), I32).at[flat_slot].set(tok, mode="drop")
    scale_of_slot = jnp.zeros((n_slots,), F32).at[flat_slot].set(gates[:, :n_experts].reshape(-1), mode="drop")
    slot_a = jnp.min(jnp.where(seli > 0, slot, n_slots), axis=1).astype(I32)
    slot_b = jnp.max(jnp.where(seli > 0, slot, -1), axis=1).astype(I32)
    tile_start = jnp.arange(n_slots // tm, dtype=I32) * tm
    tile_expert = jnp.minimum(jnp.searchsorted(ends, tile_start, side="right"), n_experts - 1).astype(I32)
    return token_of_slot, scale_of_slot.reshape(n_slots, 1), slot_a, slot_b, tile_expert


def _rope_tables(positions, rot_dim):
    half = rot_dim // 2
    inv_freq = 1.0 / (ROPE_THETA ** (jnp.arange(0, rot_dim, 2, dtype=F32) / rot_dim))
    ang = positions.astype(F32).reshape(-1, 1) * inv_freq
    cos, sin = jnp.cos(ang), jnp.sin(ang)
    t = ang.shape[0]
    ones = jnp.ones((t, LANE - rot_dim), F32)
    zeros = jnp.zeros((t, LANE - half), F32)
    c = jnp.concatenate([cos, cos, ones], axis=1)
    s1 = jnp.concatenate([-sin, zeros], axis=1)
    s2 = jnp.concatenate([zeros[:, :half], sin, zeros[:, :LANE - rot_dim]], axis=1)
    return c, s1, s2


def kernel(x, c, positions, ada_w, ada_b, norm1_w, norm2_w, attn_w_in, attn_lambda, attn_subln, attn_w_out, gdn_w_in, gdn_conv_w, gdn_a_log, gdn_dt_bias, gdn_norm_w, gdn_w_out, ffn_w_gate, ffn_w_up, ffn_w_down, moe_router, moe_w_gate, moe_w_up, moe_w_down, final_norm_w):
    batch, seq, d = x.shape
    t = batch * seq
    depth = ada_w.shape[0]
    n_experts = moe_router.shape[-1]
    dh = d // DA_HEADS // 2
    key_dim = d
    val_dim = gdn_w_out.shape[1]
    conv_dim = gdn_conv_w.shape[-1]
    hv = gdn_a_log.shape[-1]
    moe_tm = 512
    s_sub = d // LANE

    mod = _adaln(c, ada_w, ada_b)
    rope = _rope_tables(positions, dh // 4)
    h = x.reshape(t, d)
    delta = gate = None
    zero_tiles = jnp.zeros((t // moe_tm,), I32)

    for i in range(depth):
        j = i // 2
        sh1, sc1, g1, sh2, sc2, g2 = [mod[i, :, n * d:(n + 1) * d].reshape(batch, 1, d) for n in range(6)]
        if delta is None:
            u = _norm(h, norm1_w[i], seq=seq, sc=sc1, sh=sh1)[0]
        else:
            h, u = _norm(h, norm1_w[i], seq=seq, delta=delta, gate=gate, sc=sc1, sh=sh1)
            delta = gate = None
        if i % 2 == 0:
            qkv = _matmul(u, attn_w_in[j].astype(BF16), mode="rope", rope=rope, rope_cols=(d, 2 * d),
                          rope_half=dh // 8, q_scale=dh ** -0.5)
            o = _diff_attention(qkv, attn_lambda[j], attn_subln[j], batch=batch, seq=seq, layer_idx=i)
            h = _matmul(o, attn_w_out[j].astype(BF16), mode="resid", seq=seq, h=h, gate=g1, out_dtype=F32)
        else:
            w_in = gdn_w_in[j]
            n_main = conv_dim + val_dim
            proj = _matmul(u, w_in[:, :n_main].astype(BF16))
            w_ba = jnp.zeros((d, 2 * LANE), F32)
            w_ba = w_ba.at[:, :hv].set(w_in[:, n_main:n_main + hv]).at[:, LANE:LANE + hv].set(w_in[:, n_main + hv:])
            ba = _matmul(u, w_ba.astype(BF16), out_dtype=F32, tn=2 * LANE)
            beta, gc = _gdn_gates(ba, gdn_a_log[j], gdn_dt_bias[j])
            gct = gc[:, :hv].T.reshape(hv, 1, t)
            q = _conv_silu(proj, gdn_conv_w[j], col0=0, width=key_dim, mode="q", seq=seq)
            k = _conv_silu(proj, gdn_conv_w[j], col0=key_dim, width=key_dim, mode="k", seq=seq)
            v = _conv_silu(proj, gdn_conv_w[j], col0=2 * key_dim, width=val_dim, mode="v", seq=seq)
            o = _delta_rule(q, k, v, proj, conv_dim, beta, gc, gct, gdn_norm_w[j], batch=batch, seq=seq)
            h = _matmul(o, gdn_w_out[j].astype(BF16), mode="resid", seq=seq, h=h, gate=g1, out_dtype=F32)
        if i % 2 == 0:
            u = _norm(h, norm2_w[i], seq=seq, sc=sc2, sh=sh2)[0]
            h = _ffn(u, ffn_w_gate[j].astype(BF16)[None], ffn_w_up[j].astype(BF16)[None],
                     ffn_w_down[j].astype(BF16)[None], zero_tiles, tm=moe_tm, mode="resid", seq=seq, h=h, gate=g2)
        else:
            router = jnp.pad(moe_router[j], ((0, 0), (0, LANE - n_experts)))
            u, gates, sel = _norm(h, norm2_w[i], seq=seq, sc=sc2, sh=sh2, router=router, n_experts=n_experts)
            token_of_slot, scale_of_slot, slot_a, slot_b, tile_expert = _dispatch_plan(sel, gates, n_experts, moe_tm)
            xs = _gather_rows(u.reshape(t, s_sub, LANE), token_of_slot, tm=moe_tm)
            ys = _ffn(xs.reshape(-1, d), moe_w_gate[j].astype(BF16), moe_w_up[j].astype(BF16),
                      moe_w_down[j].astype(BF16), tile_expert, tm=moe_tm, mode="scale", row_scale=scale_of_slot)
            delta = _combine_rows(ys.reshape(-1, s_sub, LANE), slot_a, slot_b, tm=moe_tm).reshape(t, d)
            gate = g2
    if delta is None:
        out = _norm(h, final_norm_w, seq=seq, out_dtype=F32)[0]
    else:
        out = _norm(h, final_norm_w, seq=seq, delta=delta, gate=gate, out_dtype=F32)[1]
    return out.reshape(batch, seq, d)
```

```python
import functools
import math

import jax
import jax.numpy as jnp
from jax import lax
from jax.experimental import pallas as pl
from jax.experimental.pallas import tpu as pltpu

F32 = jnp.float32
BF16 = jnp.bfloat16
I32 = jnp.int32

LANE = 128
MXU_N = 256
V7X_VMEM_BYTES = 64 * 1024 * 1024
VMEM_LIMIT = V7X_VMEM_BYTES * 3 // 4

RMS_EPS = 1e-6
DA_HEADS = 8
ROPE_THETA = 500000.0
GDN_HEAD_DIM = 128
GDN_CHUNK = 64
L2_EPS = 1e-6
TOP_K = 2
NEG = -1e30


def _cparams(*sem):
    return pltpu.CompilerParams(dimension_semantics=sem, vmem_limit_bytes=VMEM_LIMIT)


def _silu(x):
    return x / (1.0 + jnp.exp(-x))


def _adaln_kernel(c_ref, w_ref, b_ref, o_ref):
    ca = _silu(c_ref[...]).astype(BF16)
    o_ref[0] = jnp.dot(ca, w_ref[0].astype(BF16), preferred_element_type=F32) + b_ref[0]


def _adaln(c, ada_w, ada_b):
    depth, d, n = ada_w.shape
    b = c.shape[0]
    tn = 1024
    return pl.pallas_call(
        _adaln_kernel,
        grid=(depth, n // tn),
        in_specs=[pl.BlockSpec((b, d), lambda l, j: (0, 0)),
                  pl.BlockSpec((1, d, tn), lambda l, j: (l, 0, j)),
                  pl.BlockSpec((1, 1, tn), lambda l, j: (l, 0, j))],
        out_specs=pl.BlockSpec((1, b, tn), lambda l, j: (l, 0, j)),
        out_shape=jax.ShapeDtypeStruct((depth, b, n), F32),
        compiler_params=_cparams("parallel", "parallel"), name="adaln",
    )(c, ada_w, ada_b.reshape(depth, 1, n))


def _norm_kernel(*refs, has_delta, has_mod, n_experts):
    refs = list(refs)
    h_ref = refs.pop(0)
    if has_delta:
        d_ref, g_ref = refs.pop(0), refs.pop(0)
    w_ref = refs.pop(0)
    if has_mod:
        sc_ref, sh_ref = refs.pop(0), refs.pop(0)
    if n_experts:
        r_ref = refs.pop(0)
    if has_delta:
        hn_ref = refs.pop(0)
    u_ref = refs.pop(0)
    if n_experts:
        gates_ref, sel_ref = refs.pop(0), refs.pop(0)

    x = h_ref[...]
    if has_delta:
        x = x + g_ref[0] * d_ref[...]
        hn_ref[...] = x
    y = x * lax.rsqrt(jnp.mean(x * x, axis=-1, keepdims=True) + RMS_EPS) * w_ref[...]
    if has_mod:
        y = y * (1.0 + sc_ref[0]) + sh_ref[0]
    u_ref[...] = y.astype(u_ref.dtype)
    if n_experts:
        logits = jnp.dot(y, r_ref[...], preferred_element_type=F32, precision=lax.Precision.HIGHEST)
        lane = lax.broadcasted_iota(I32, logits.shape, 1)
        logits = jnp.where(lane < n_experts, logits, -jnp.inf)
        m1 = jnp.max(logits, axis=1, keepdims=True)
        i1 = jnp.min(jnp.where(logits == m1, lane, LANE), axis=1, keepdims=True)
        oh1 = lane == i1
        rest = jnp.where(oh1, -jnp.inf, logits)
        m2 = jnp.max(rest, axis=1, keepdims=True)
        i2 = jnp.min(jnp.where(rest == m2, lane, LANE), axis=1, keepdims=True)
        oh2 = lane == i2
        e = jnp.exp(m2 - m1)
        gates_ref[...] = jnp.where(oh1, 1.0 / (1.0 + e), 0.0) + jnp.where(oh2, e / (1.0 + e), 0.0)
        sel_ref[...] = jnp.where(oh1 | oh2, 1.0, 0.0)


def _norm(h, w, *, seq, delta=None, gate=None, sc=None, sh=None, router=None, n_experts=0,
          out_dtype=BF16):
    t, d = h.shape
    tm = 256
    per_seq = seq // tm
    row = pl.BlockSpec((tm, d), lambda i: (i, 0))
    vec = pl.BlockSpec((1, d), lambda i: (0, 0))
    bvec = pl.BlockSpec((1, 1, d), lambda i: (i // per_seq, 0, 0))
    args, in_specs = [h], [row]
    if delta is not None:
        args += [delta, gate]
        in_specs += [row, bvec]
    args.append(w.reshape(1, d))
    in_specs.append(vec)
    if sc is not None:
        args += [sc, sh]
        in_specs += [bvec, bvec]
    if router is not None:
        args.append(router)
        in_specs.append(pl.BlockSpec((d, LANE), lambda i: (0, 0)))
    out_shape, out_specs = [], []
    if delta is not None:
        out_shape.append(jax.ShapeDtypeStruct((t, d), F32))
        out_specs.append(row)
    out_shape.append(jax.ShapeDtypeStruct((t, d), out_dtype))
    out_specs.append(row)
    if router is not None:
        out_shape += [jax.ShapeDtypeStruct((t, LANE), F32)] * 2
        out_specs += [pl.BlockSpec((tm, LANE), lambda i: (i, 0))] * 2
    kern = functools.partial(_norm_kernel, has_delta=delta is not None, has_mod=sc is not None,
                             n_experts=n_experts)
    return pl.pallas_call(kern, grid=(t // tm,), in_specs=in_specs, out_specs=out_specs,
                          out_shape=out_shape, compiler_params=_cparams("parallel"), name="norm")(*args)


def _rope_group(x, c, s1, s2, half):
    return x * c + pltpu.roll(x, LANE - half, axis=1) * s1 + pltpu.roll(x, half, axis=1) * s2


def _mm_kernel(*refs, mode, n_q, n_qk, q_scale, rope_half):
    if mode == "rope":
        a_ref, w_ref, c_ref, s1_ref, s2_ref, o_ref = refs
    elif mode == "resid":
        a_ref, w_ref, h_ref, g_ref, o_ref = refs
    else:
        a_ref, w_ref, o_ref = refs
    acc = jnp.dot(a_ref[...], w_ref[...], preferred_element_type=F32)
    if mode == "resid":
        o_ref[...] = h_ref[...] + g_ref[0] * acc
    elif mode == "rope":
        j = pl.program_id(1)
        tn = acc.shape[1]

        @pl.when(j < n_qk)
        def _():
            c, s1, s2 = c_ref[...], s1_ref[...], s2_ref[...]
            scale = jnp.where(j < n_q, q_scale, 1.0)
            for g in range(tn // LANE):
                sl = slice(g * LANE, (g + 1) * LANE)
                o_ref[:, sl] = (_rope_group(acc[:, sl], c, s1, s2, rope_half) * scale).astype(o_ref.dtype)

        @pl.when(j >= n_qk)
        def _():
            o_ref[...] = acc.astype(o_ref.dtype)
    else:
        o_ref[...] = acc.astype(o_ref.dtype)


def _matmul(a, w, *, out_dtype=BF16, mode="plain", seq=None, h=None, gate=None, rope=None,
            rope_cols=(0, 0), rope_half=0, q_scale=1.0, tm=1024, tn=512):
    m, k = a.shape
    n = w.shape[1]
    tm, tn = min(tm, m, seq or m), min(tn, n)
    in_specs = [pl.BlockSpec((tm, k), lambda i, j: (i, 0)), pl.BlockSpec((k, tn), lambda i, j: (0, j))]
    args = [a, w]
    if mode == "rope":
        tab = pl.BlockSpec((tm, LANE), lambda i, j: (i, 0))
        in_specs += [tab, tab, tab]
        args += list(rope)
    elif mode == "resid":
        per_seq = seq // tm
        in_specs += [pl.BlockSpec((tm, tn), lambda i, j: (i, j)),
                     pl.BlockSpec((1, 1, tn), lambda i, j: (i // per_seq, 0, j))]
        args += [h, gate]
    kern = functools.partial(_mm_kernel, mode=mode, n_q=rope_cols[0] // tn, n_qk=rope_cols[1] // tn,
                             q_scale=q_scale, rope_half=rope_half)
    return pl.pallas_call(
        kern, grid=(m // tm, n // tn), in_specs=in_specs,
        out_specs=pl.BlockSpec((tm, tn), lambda i, j: (i, j)),
        out_shape=jax.ShapeDtypeStruct((m, n), out_dtype),
        compiler_params=_cparams("parallel", "parallel"), name="proj_" + mode,
    )(*args)


def _attn_kernel(lam_ref, q_ref, k_ref, v_ref, sub_ref, o_ref, m_ref, l_ref, acc_ref, *,
                 blk, dh, lambda_init):
    qi, ki = pl.program_id(2), pl.program_id(3)

    @pl.when(ki == 0)
    def _():
        m_ref[...] = jnp.full(m_ref.shape, NEG, F32)
        l_ref[...] = jnp.zeros(l_ref.shape, F32)
        acc_ref[...] = jnp.zeros(acc_ref.shape, F32)

    def step(masked):
        v = v_ref[...]
        n_t = blk // LANE
        if masked:
            row = lax.broadcasted_iota(I32, (blk, LANE), 0)
            col = lax.broadcasted_iota(I32, (blk, LANE), 1)
        for j in range(2):
            sl = slice(j * dh, (j + 1) * dh)
            s = lax.dot_general(q_ref[:, sl], k_ref[:, sl], (((1,), (1,)), ((), ())),
                                preferred_element_type=F32)
            tiles = [s[:, n * LANE:(n + 1) * LANE] for n in range(n_t)]
            if masked:
                tiles = [jnp.where(col + n * LANE <= row, tiles[n], NEG) for n in range(n_t)]
            mx = tiles[0]
            for tl in tiles[1:]:
                mx = jnp.maximum(mx, tl)
            m_prev = m_ref[j]
            m_new = jnp.maximum(m_prev, jnp.max(mx, axis=1, keepdims=True))
            alpha = jnp.exp2(m_prev - m_new)
            ps = [jnp.exp2(tl - m_new) for tl in tiles]
            psum = ps[0]
            for pt in ps[1:]:
                psum = psum + pt
            l_ref[j] = alpha * l_ref[j] + psum
            pv = jnp.dot(jnp.concatenate(ps, axis=1).astype(BF16), v, preferred_element_type=F32)
            for n in range(pv.shape[1] // LANE):
                vl = slice(n * LANE, (n + 1) * LANE)
                acc_ref[j, :, vl] = alpha * acc_ref[j, :, vl] + pv[:, vl]
            m_ref[j] = m_new

    @pl.when(ki < qi)
    def _():
        step(False)

    @pl.when(ki == qi)
    def _():
        step(True)

    @pl.when(ki == pl.num_programs(3) - 1)
    def _():
        lv = lam_ref[...]
        lam = (jnp.exp(jnp.sum(lv[0:1] * lv[1:2], axis=1, keepdims=True))
               - jnp.exp(jnp.sum(lv[2:3] * lv[3:4], axis=1, keepdims=True)) + lambda_init)
        l0 = jnp.sum(l_ref[0], axis=1, keepdims=True)
        l1 = jnp.sum(l_ref[1], axis=1, keepdims=True)
        o = acc_ref[0] / l0 - lam * (acc_ref[1] / l1)
        o = o * lax.rsqrt(jnp.mean(o * o, axis=-1, keepdims=True) + RMS_EPS) * sub_ref[...]
        o_ref[...] = (o * (1.0 - lambda_init)).astype(o_ref.dtype)


def _diff_attention(qkv, lam_vec, subln, *, batch, seq, layer_idx):
    t, d3 = qkv.shape
    d = d3 // 3
    dv = d // DA_HEADS
    dh = dv // 2
    blk = 512
    nb = seq // blk
    lambda_init = 0.8 - 0.6 * math.exp(-0.3 * layer_idx)
    kern = functools.partial(_attn_kernel, blk=blk, dh=dh, lambda_init=lambda_init)
    return pl.pallas_call(
        kern, grid=(batch, DA_HEADS, nb, nb),
        in_specs=[pl.BlockSpec((4, dh), lambda b, h, qi, ki: (0, 0)),
                  pl.BlockSpec((blk, dv), lambda b, h, qi, ki: (b * nb + qi, h)),
                  pl.BlockSpec((blk, dv), lambda b, h, qi, ki: (b * nb + jnp.minimum(ki, qi), DA_HEADS + h)),
                  pl.BlockSpec((blk, dv), lambda b, h, qi, ki: (b * nb + jnp.minimum(ki, qi), 2 * DA_HEADS + h)),
                  pl.BlockSpec((1, dv), lambda b, h, qi, ki: (0, 0))],
        out_specs=pl.BlockSpec((blk, dv), lambda b, h, qi, ki: (b * nb + qi, h)),
        out_shape=jax.ShapeDtypeStruct((t, d), BF16),
        scratch_shapes=[pltpu.VMEM((2, blk, LANE), F32), pltpu.VMEM((2, blk, LANE), F32),
                        pltpu.VMEM((2, blk, dv), F32)],
        compiler_params=_cparams("parallel", "parallel", "parallel", "arbitrary"), name="diff_attn",
    )(lam_vec, qkv, qkv, qkv, subln.reshape(1, dv))


def _conv_kernel(x_ref, halo_ref, w_ref, o_ref, *, mode, per_seq, kw):
    i = pl.program_id(0)
    x = x_ref[...].astype(F32)
    halo = jnp.where(i % per_seq == 0, 0.0, halo_ref[8:, :].astype(F32))
    w = w_ref[...]
    head = jnp.concatenate([halo, x[:8]], axis=0)
    y = x * w[kw - 1:kw]
    y_head = head * w[kw - 1:kw]
    for s in range(1, kw):
        y = y + pltpu.roll(x, s, axis=0) * w[kw - 1 - s:kw - s]
        y_head = y_head + pltpu.roll(head, s, axis=0) * w[kw - 1 - s:kw - s]
    y = jnp.concatenate([y_head[8:], y[8:]], axis=0)
    y = _silu(y)
    if mode == "v":
        o_ref[...] = y.astype(o_ref.dtype)
    else:
        scale = GDN_HEAD_DIM ** -0.5 if mode == "q" else 1.0
        for g in range(y.shape[1] // GDN_HEAD_DIM):
            sl = slice(g * GDN_HEAD_DIM, (g + 1) * GDN_HEAD_DIM)
            yg = y[:, sl]
            yn = yg * lax.rsqrt(jnp.sum(yg * yg, axis=-1, keepdims=True) + L2_EPS)
            o_ref[:, sl] = (yn * scale).astype(o_ref.dtype)


def _conv_silu(proj, conv_w, *, col0, width, mode, seq):
    t = proj.shape[0]
    kw = conv_w.shape[0]
    tm, tc = 512, 512
    cb = col0 // tc
    per_seq = seq // tm
    kern = functools.partial(_conv_kernel, mode=mode, per_seq=per_seq, kw=kw)
    return pl.pallas_call(
        kern, grid=(t // tm, width // tc),
        in_specs=[pl.BlockSpec((tm, tc), lambda i, j: (i, cb + j)),
                  pl.BlockSpec((16, tc), lambda i, j: (jnp.maximum(i * (tm // 16) - 1, 0), cb + j)),
                  pl.BlockSpec((kw, tc), lambda i, j: (0, cb + j))],
        out_specs=pl.BlockSpec((tm, tc), lambda i, j: (i, j)),
        out_shape=jax.ShapeDtypeStruct((t, width), BF16),
        compiler_params=_cparams("parallel", "parallel"), name="conv_" + mode,
    )(proj, proj, conv_w)


def _split3(x):
    hi = x.astype(BF16)
    r = x - hi.astype(F32)
    mid = r.astype(BF16)
    lo = (r - mid.astype(F32)).astype(BF16)
    return hi, mid, lo


def _gates_kernel(ba_ref, alog_ref, dt_ref, beta_ref, gc_ref):
    x = ba_ref[...]
    tm = x.shape[0]
    b_raw, a_raw = x[:, :LANE], x[:, LANE:]
    beta_ref[...] = 1.0 / (1.0 + jnp.exp(-b_raw))
    z = a_raw + dt_ref[...]
    softplus = jnp.maximum(z, 0.0) + jnp.log(1.0 + jnp.exp(-jnp.abs(z)))
    g = -jnp.exp(alog_ref[...]) * softplus
    row = lax.broadcasted_iota(I32, (tm, tm), 0)
    col = lax.broadcasted_iota(I32, (tm, tm), 1)
    tri = jnp.where((row >= col) & (row // GDN_CHUNK == col // GDN_CHUNK), 1.0, 0.0).astype(BF16)
    gc_ref[...] = sum(jnp.dot(tri, part, preferred_element_type=F32) for part in _split3(g))


def _gdn_gates(ba, a_log, dt_bias):
    t = ba.shape[0]
    hv = a_log.shape[0]
    tm = 256
    pad = lambda v: jnp.pad(v.astype(F32), (0, LANE - hv)).reshape(1, LANE)
    blk = pl.BlockSpec((tm, LANE), lambda i: (i, 0))
    vec = pl.BlockSpec((1, LANE), lambda i: (0, 0))
    return pl.pallas_call(
        _gates_kernel, grid=(t // tm,),
        in_specs=[pl.BlockSpec((tm, 2 * LANE), lambda i: (i, 0)), vec, vec],
        out_specs=[blk, blk],
        out_shape=[jax.ShapeDtypeStruct((t, LANE), F32)] * 2,
        compiler_params=_cparams("parallel"), name="gdn_gates",
    )(ba, pad(a_log), pad(dt_bias))


def _delta_kernel(q_ref, k_ref, v_ref, z_ref, beta_ref, gc_ref, gct_ref, nw_ref, o_ref, state_ref, *,
                  heads, rows):
    hg, blk = pl.program_id(1), pl.program_id(2)
    dk = GDN_HEAD_DIM
    c = GDN_CHUNK
    n_chunks = rows // c

    @pl.when(blk == 0)
    def _():
        state_ref[...] = jnp.zeros(state_ref.shape, F32)

    n_pairs = rows // LANE
    ri = lax.broadcasted_iota(I32, (c, LANE), 0)
    li = lax.broadcasted_iota(I32, (c, LANE), 1)
    lj = li & (c - 1)
    left = li < c
    strict = ri > lj
    causal = ri >= lj
    same16 = (ri >> 4) == (lj >> 4)
    same32 = (ri >> 5) == (lj >> 5)
    m16 = strict & same16
    m32 = strict & same32 & jnp.logical_not(same16)
    m64 = strict & jnp.logical_not(same32)
    eye = jnp.where(ri == lj, 1.0, 0.0)
    bd_mask = ((lax.broadcasted_iota(I32, (LANE, LANE), 0) < c)
               == (lax.broadcasted_iota(I32, (LANE, LANE), 1) < c))
    lane = lax.broadcasted_iota(I32, (rows, LANE), 1)
    contract_last = (((1,), (1,)), ((), ()))
    contract_first = (((0,), (0,)), ((), ()))

    def bd(p):
        return jnp.where(bd_mask, jnp.concatenate([p, p], axis=0), 0.0).astype(BF16)

    def pm(p, q):
        return jnp.dot(p.astype(BF16), bd(q), preferred_element_type=F32)

    def pack(full):
        return jnp.where(left, full[:c], full[c:])

    gc_all, beta_all = gc_ref[...], beta_ref[...]
    eg_all = jnp.exp(gc_all)
    pairs = [(g, p) for g in range(heads) for p in range(n_pairs)]
    prow = lambda x, p: x[p * LANE:(p + 1) * LANE]

    kkt, qkt = {}, {}
    for kh in range(heads // 2):
        ks = slice(kh * dk, (kh + 1) * dk)
        for p in range(n_pairs):
            kp = k_ref[p * LANE:(p + 1) * LANE, ks]
            qp = q_ref[p * LANE:(p + 1) * LANE, ks]
            kkt[kh, p] = pack(lax.dot_general(kp, kp, contract_last, preferred_element_type=F32))
            qkt[kh, p] = pack(lax.dot_general(qp, kp, contract_last, preferred_element_type=F32))

    hd = []
    for g in range(heads):
        pick = lane == hg * heads + g
        col = lambda x: jnp.sum(jnp.where(pick, x, 0.0), axis=1, keepdims=True)
        gc_col, beta_col, eg_col = col(gc_all), col(beta_all), col(eg_all)
        ks = slice((g // 2) * dk, (g // 2 + 1) * dk)
        kf = k_ref[:, ks].astype(F32)
        k_beta = kf * beta_col
        y = jnp.concatenate([v_ref[:, g * dk:(g + 1) * dk].astype(F32) * beta_col, k_beta * eg_col],
                            axis=1).astype(BF16)
        gl = [gc_col[ci * c + c - 1:ci * c + c] for ci in range(n_chunks)]
        gl_col = jnp.concatenate([jnp.broadcast_to(v, (c, 1)) for v in gl], axis=0)
        hd.append(dict(gc=gc_col, beta=beta_col, y=y,
                       q_dec=(q_ref[:, ks].astype(F32) * eg_col).astype(BF16),
                       k_dec=(kf * jnp.exp(gl_col - gc_col)).astype(BF16),
                       g_end=[jnp.exp(v) for v in gl]))

    a, intra = {}, {}
    for g, p in pairs:
        gcp = jnp.where(left, prow(hd[g]["gc"], p)[:c], prow(hd[g]["gc"], p)[c:])
        bp = jnp.where(left, prow(hd[g]["beta"], p)[:c], prow(hd[g]["beta"], p)[c:])
        decay = jnp.exp(jnp.where(causal, gcp - gct_ref[g][:, p * LANE:(p + 1) * LANE], NEG))
        a[g, p] = kkt[g // 2, p] * bp * decay
        intra[g, p] = qkt[g // 2, p] * decay

    d1 = {n: jnp.where(m16, a[n], 0.0) for n in pairs}
    d2 = {n: pm(d1[n], d1[n]) for n in pairs}
    d4 = {n: pm(d2[n], d2[n]) for n in pairs}
    x = {n: eye - d1[n] for n in pairs}
    x = {n: x[n] + pm(x[n], d2[n]) for n in pairs}
    d8 = {n: pm(d4[n], d4[n]) for n in pairs}
    x = {n: x[n] + pm(x[n], d4[n]) for n in pairs}
    x = {n: x[n] + pm(x[n], d8[n]) for n in pairs}
    for mask in (m32, m64):
        t = {n: pm(jnp.where(mask, a[n], 0.0), x[n]) for n in pairs}
        x = {n: x[n] - pm(x[n], t[n]) for n in pairs}
    uw = {(g, p): jnp.dot(bd(x[g, p]), prow(hd[g]["y"], p), preferred_element_type=F32) for g, p in pairs}

    s = [state_ref[g] for g in range(heads)]
    v_new = {}
    o_state = {}
    for ci in range(n_chunks):
        p, half = divmod(ci, 2)
        hs = slice(half * c, (half + 1) * c)
        rs = slice(ci * c, (ci + 1) * c)
        sb = [s[g].astype(BF16) for g in range(heads)]
        for g in range(heads):
            u_c, w_c = uw[g, p][hs, :dk], uw[g, p][hs, dk:]
            v_new[g, ci] = u_c - jnp.dot(w_c.astype(BF16), sb[g], preferred_element_type=F32)
            o_state[g, ci] = jnp.dot(hd[g]["q_dec"][rs], sb[g], preferred_element_type=F32)
        for g in range(heads):
            s[g] = s[g] * hd[g]["g_end"][ci] + lax.dot_general(
                hd[g]["k_dec"][rs], v_new[g, ci].astype(BF16), contract_first, preferred_element_type=F32)
    for g in range(heads):
        state_ref[g] = s[g]

    for g in range(heads):
        o_pairs = []
        for p in range(n_pairs):
            vn = jnp.concatenate([v_new[g, 2 * p], v_new[g, 2 * p + 1]], axis=0).astype(BF16)
            o_in = jnp.dot(bd(intra[g, p]), vn, preferred_element_type=F32)
            o_pairs.append(o_in + jnp.concatenate([o_state[g, 2 * p], o_state[g, 2 * p + 1]], axis=0))
        o = jnp.concatenate(o_pairs, axis=0)
        o = o * lax.rsqrt(jnp.mean(o * o, axis=-1, keepdims=True) + RMS_EPS) * nw_ref[...]
        vs = slice(g * dk, (g + 1) * dk)
        o_ref[:, vs] = (o * _silu(z_ref[:, vs].astype(F32))).astype(o_ref.dtype)


def _delta_rule(q, k, v, proj, z_col0, beta, gc, gct, norm_w, *, batch, seq):
    t, vdim = v.shape
    dk = GDN_HEAD_DIM
    hv = vdim // dk
    heads = 8
    rows = 256
    nblk = seq // rows
    qw, vw = heads // 2 * dk, heads * dk
    zb = z_col0 // vw
    kern = functools.partial(_delta_kernel, heads=heads, rows=rows)
    tok = lambda b, hg, i: (b * nblk + i, 0)
    return pl.pallas_call(
        kern, grid=(batch, hv // heads, nblk),
        in_specs=[pl.BlockSpec((rows, qw), lambda b, hg, i: (b * nblk + i, hg)),
                  pl.BlockSpec((rows, qw), lambda b, hg, i: (b * nblk + i, hg)),
                  pl.BlockSpec((rows, vw), lambda b, hg, i: (b * nblk + i, hg)),
                  pl.BlockSpec((rows, vw), lambda b, hg, i: (b * nblk + i, zb + hg)),
                  pl.BlockSpec((rows, LANE), tok),
                  pl.BlockSpec((rows, LANE), tok),
                  pl.BlockSpec((heads, 1, rows), lambda b, hg, i: (hg, 0, b * nblk + i)),
                  pl.BlockSpec((1, dk), lambda b, hg, i: (0, 0))],
        out_specs=pl.BlockSpec((rows, vw), lambda b, hg, i: (b * nblk + i, hg)),
        out_shape=jax.ShapeDtypeStruct((t, vdim), BF16),
        scratch_shapes=[pltpu.VMEM((heads, dk, dk), F32)],
        compiler_params=_cparams("parallel", "parallel", "arbitrary"), name="delta_rule",
    )(q, k, v, proj, beta, gc, gct, norm_w.reshape(1, dk))


def _ffn_kernel(te_ref, *refs, mode):
    del te_ref
    if mode == "resid":
        x_ref, wg_ref, wu_ref, wd_ref, h_ref, g_ref, o_ref, acc_ref = refs
    else:
        x_ref, wg_ref, wu_ref, wd_ref, o_ref, acc_ref = refs
    f = pl.program_id(1)

    @pl.when(f == 0)
    def _():
        acc_ref[...] = jnp.zeros(acc_ref.shape, F32)

    x = x_ref[...]
    tf = wg_ref.shape[2]
    part = None
    for n in range(tf // MXU_N):
        cs = slice(n * MXU_N, (n + 1) * MXU_N)
        gate = jnp.dot(x, wg_ref[0, :, cs], preferred_element_type=F32)
        up = jnp.dot(x, wu_ref[0, :, cs], preferred_element_type=F32)
        dn = jnp.dot((_silu(gate) * up).astype(BF16), wd_ref[0, cs, :], preferred_element_type=F32)
        part = dn if part is None else part + dn
    acc_ref[...] += part

    @pl.when(f == pl.num_programs(1) - 1)
    def _():
        if mode == "resid":
            o_ref[...] = h_ref[...] + g_ref[0] * acc_ref[...]
        else:
            o_ref[...] = acc_ref[...].astype(o_ref.dtype)


def _ffn(x, wg, wu, wd, tile_expert, *, tm, mode, seq=None, h=None, gate=None):
    m, d = x.shape
    ff = wg.shape[2]
    tf = 512
    xs = pl.BlockSpec((tm, d), lambda i, f, te: (i, 0))
    in_specs = [xs,
                pl.BlockSpec((1, d, tf), lambda i, f, te: (te[i], 0, f)),
                pl.BlockSpec((1, d, tf), lambda i, f, te: (te[i], 0, f)),
                pl.BlockSpec((1, tf, d), lambda i, f, te: (te[i], f, 0))]
    args = [x, wg, wu, wd]
    if mode == "resid":
        per_seq = seq // tm
        in_specs += [xs, pl.BlockSpec((1, 1, d), lambda i, f, te: (i // per_seq, 0, 0))]
        args += [h, gate]
        out_dtype = F32
    else:
        out_dtype = BF16
    return pl.pallas_call(
        functools.partial(_ffn_kernel, mode=mode),
        grid_spec=pltpu.PrefetchScalarGridSpec(
            num_scalar_prefetch=1, grid=(m // tm, ff // tf), in_specs=in_specs, out_specs=xs,
            scratch_shapes=[pltpu.VMEM((tm, d), F32)]),
        out_shape=jax.ShapeDtypeStruct((m, d), out_dtype),
        compiler_params=_cparams("parallel", "arbitrary"), name="swiglu_" + mode,
    )(tile_expert, *args)


def _scatter_kernel(sa_ref, sb_ref, x_ref, init_ref, o_ref, sem, *, tm):
    del init_ref
    def issue(r, carry):
        pltpu.make_async_copy(x_ref.at[r], o_ref.at[sa_ref[r]], sem).start()
        pltpu.make_async_copy(x_ref.at[r], o_ref.at[sb_ref[r]], sem).start()
        return carry
    lax.fori_loop(0, tm, issue, 0)
    drain = pltpu.make_async_copy(x_ref, o_ref.at[pl.ds(0, tm)], sem)
    drain.wait()
    drain.wait()


def _scatter_rows(x, slot_a, slot_b, n_slots, *, tm):
    t, s, l = x.shape
    ispec = pl.BlockSpec((tm,), lambda i: (i,), memory_space=pltpu.SMEM)
    return pl.pallas_call(
        functools.partial(_scatter_kernel, tm=tm), grid=(t // tm,),
        in_specs=[ispec, ispec, pl.BlockSpec((tm, s, l), lambda i: (i, 0, 0)), pl.BlockSpec(memory_space=pl.ANY)],
        out_specs=pl.BlockSpec(memory_space=pl.ANY),
        out_shape=jax.ShapeDtypeStruct((n_slots, s, l), x.dtype),
        scratch_shapes=[pltpu.SemaphoreType.DMA(())],
        input_output_aliases={3: 0},
        compiler_params=_cparams("arbitrary"), name="scatter_rows",
    )(slot_a, slot_b, x, jnp.zeros((n_slots, s, l), x.dtype))


def _combine_kernel(ia_ref, ib_ref, wa_ref, wb_ref, src_ref, o_ref, a_buf, b_buf, sem_a, sem_b, *, tm):
    def issue(r, carry):
        pltpu.make_async_copy(src_ref.at[ia_ref[r]], a_buf.at[r], sem_a).start()
        pltpu.make_async_copy(src_ref.at[ib_ref[r]], b_buf.at[r], sem_b).start()
        return carry
    lax.fori_loop(0, tm, issue, 0)
    pltpu.make_async_copy(src_ref.at[pl.ds(0, tm)], a_buf, sem_a).wait()
    pltpu.make_async_copy(src_ref.at[pl.ds(0, tm)], b_buf, sem_b).wait()
    o_ref[...] = wa_ref[...] * a_buf[...].astype(F32) + wb_ref[...] * b_buf[...].astype(F32)


def _combine_rows(src, idx_a, idx_b, w_a, w_b, *, tm):
    _, s, l = src.shape
    t = idx_a.shape[0]
    ispec = pl.BlockSpec((tm,), lambda i: (i,), memory_space=pltpu.SMEM)
    wspec = pl.BlockSpec((tm, 1, l), lambda i: (i, 0, 0))
    return pl.pallas_call(
        functools.partial(_combine_kernel, tm=tm), grid=(t // tm,),
        in_specs=[ispec, ispec, wspec, wspec, pl.BlockSpec(memory_space=pl.ANY)],
        out_specs=pl.BlockSpec((tm, s, l), lambda i: (i, 0, 0)),
        out_shape=jax.ShapeDtypeStruct((t, s, l), F32),
        scratch_shapes=[pltpu.VMEM((tm, s, l), src.dtype), pltpu.VMEM((tm, s, l), src.dtype),
                        pltpu.SemaphoreType.DMA(()), pltpu.SemaphoreType.DMA(())],
        compiler_params=_cparams("arbitrary"), name="combine_rows",
    )(idx_a, idx_b, w_a, w_b, src)


def _dispatch_plan(sel, gates, n_experts, tm):
    t = sel.shape[0]
    chosen = sel[:, :n_experts] > 0.0
    seli = chosen.astype(I32)
    counts = jnp.sum(seli, axis=0)
    padded = (counts + tm - 1) // tm * tm
    ends = jnp.cumsum(padded)
    starts = ends - padded
    rank = jnp.cumsum(seli, axis=0) - seli
    slot = starts[None, :] + rank
    n_slots = t * TOP_K + n_experts * tm
    slot_a = jnp.min(jnp.where(chosen, slot, n_slots), axis=1).astype(I32)
    slot_b = jnp.max(jnp.where(chosen, slot, -1), axis=1).astype(I32)
    g = gates[:, :n_experts]
    lanes = lambda w: jnp.broadcast_to(w[:, None, None], (t, 1, LANE))
    w_a = lanes(jnp.sum(jnp.where(chosen & (slot == slot_a[:, None]), g, 0.0), axis=1))
    w_b = lanes(jnp.sum(jnp.where(chosen & (slot == slot_b[:, None]), g, 0.0), axis=1))
    tile_start = jnp.arange(n_slots // tm, dtype=I32) * tm
    tile_expert = jnp.minimum(jnp.searchsorted(ends, tile_start, side="right"), n_experts - 1).astype(I32)
    return slot_a, slot_b, w_a, w_b, tile_expert, n_slots


def _rope_tables(positions, rot_dim):
    half = rot_dim // 2
    inv_freq = 1.0 / (ROPE_THETA ** (jnp.arange(0, rot_dim, 2, dtype=F32) / rot_dim))
    ang = positions.astype(F32).reshape(-1, 1) * inv_freq
    cos, sin = jnp.cos(ang), jnp.sin(ang)
    t = ang.shape[0]
    ones = jnp.ones((t, LANE - rot_dim), F32)
    zeros = jnp.zeros((t, LANE - half), F32)
    c = jnp.concatenate([cos, cos, ones], axis=1)
    s1 = jnp.concatenate([-sin, zeros], axis=1)
    s2 = jnp.concatenate([zeros[:, :half], sin, zeros[:, :LANE - rot_dim]], axis=1)
    return c, s1, s2


def kernel(x, c, positions, ada_w, ada_b, norm1_w, norm2_w, attn_w_in, attn_lambda, attn_subln, attn_w_out, gdn_w_in, gdn_conv_w, gdn_a_log, gdn_dt_bias, gdn_norm_w, gdn_w_out, ffn_w_gate, ffn_w_up, ffn_w_down, moe_router, moe_w_gate, moe_w_up, moe_w_down, final_norm_w):
    batch, seq, d = x.shape
    t = batch * seq
    depth = ada_w.shape[0]
    n_experts = moe_router.shape[-1]
    dh = d // DA_HEADS // 2
    key_dim = d
    val_dim = gdn_w_out.shape[1]
    conv_dim = gdn_conv_w.shape[-1]
    hv = gdn_a_log.shape[-1]
    moe_tm = 512
    s_sub = d // LANE

    mod = _adaln(c, ada_w, ada_b)
    rope = _rope_tables(positions, dh // 4)
    h = x.reshape(t, d)
    delta = gate = None
    zero_tiles = jnp.zeros((t // moe_tm,), I32)

    for i in range(depth):
        j = i // 2
        sh1, sc1, g1, sh2, sc2, g2 = [mod[i, :, n * d:(n + 1) * d].reshape(batch, 1, d) for n in range(6)]
        if delta is None:
            u = _norm(h, norm1_w[i], seq=seq, sc=sc1, sh=sh1)[0]
        else:
            h, u = _norm(h, norm1_w[i], seq=seq, delta=delta, gate=gate, sc=sc1, sh=sh1)
            delta = gate = None
        if i % 2 == 0:
            qkv = _matmul(u, attn_w_in[j].astype(BF16), mode="rope", rope=rope, rope_cols=(d, 2 * d),
                          rope_half=dh // 8, q_scale=dh ** -0.5 * math.log2(math.e))
            o = _diff_attention(qkv, attn_lambda[j], attn_subln[j], batch=batch, seq=seq, layer_idx=i)
            h = _matmul(o, attn_w_out[j].astype(BF16), mode="resid", seq=seq, h=h, gate=g1, out_dtype=F32)
        else:
            w_in = gdn_w_in[j]
            n_main = conv_dim + val_dim
            proj = _matmul(u, w_in[:, :n_main].astype(BF16))
            w_ba = jnp.zeros((d, 2 * LANE), F32)
            w_ba = w_ba.at[:, :hv].set(w_in[:, n_main:n_main + hv]).at[:, LANE:LANE + hv].set(w_in[:, n_main + hv:])
            ba = _matmul(u, w_ba.astype(BF16), out_dtype=F32, tn=2 * LANE)
            beta, gc = _gdn_gates(ba, gdn_a_log[j], gdn_dt_bias[j])
            gct = gc[:, :hv].T.reshape(hv, 1, t)
            q = _conv_silu(proj, gdn_conv_w[j], col0=0, width=key_dim, mode="q", seq=seq)
            k = _conv_silu(proj, gdn_conv_w[j], col0=key_dim, width=key_dim, mode="k", seq=seq)
            v = _conv_silu(proj, gdn_conv_w[j], col0=2 * key_dim, width=val_dim, mode="v", seq=seq)
            o = _delta_rule(q, k, v, proj, conv_dim, beta, gc, gct, gdn_norm_w[j], batch=batch, seq=seq)
            h = _matmul(o, gdn_w_out[j].astype(BF16), mode="resid", seq=seq, h=h, gate=g1, out_dtype=F32)
        if i % 2 == 0:
            u = _norm(h, norm2_w[i], seq=seq, sc=sc2, sh=sh2)[0]
            h = _ffn(u, ffn_w_gate[j].astype(BF16)[None], ffn_w_up[j].astype(BF16)[None],
                     ffn_w_down[j].astype(BF16)[None], zero_tiles, tm=moe_tm, mode="resid", seq=seq, h=h, gate=g2)
        else:
            router = jnp.pad(moe_router[j], ((0, 0), (0, LANE - n_experts)))
            u, gates, sel = _norm(h, norm2_w[i], seq=seq, sc=sc2, sh=sh2, router=router, n_experts=n_experts)
            slot_a, slot_b, w_a, w_b, tile_expert, n_slots = _dispatch_plan(sel, gates, n_experts, moe_tm)
            xs = _scatter_rows(u.reshape(t, s_sub, LANE), slot_a, slot_b, n_slots, tm=moe_tm)
            ys = _ffn(xs.reshape(-1, d), moe_w_gate[j].astype(BF16), moe_w_up[j].astype(BF16),
                      moe_w_down[j].astype(BF16), tile_expert, tm=moe_tm, mode="plain")
            delta = _combine_rows(ys.reshape(-1, s_sub, LANE), slot_a, slot_b, w_a, w_b, tm=moe_tm).reshape(t, d)
            gate = g2
    if delta is None:
        out = _norm(h, final_norm_w, seq=seq, out_dtype=F32)[0]
    else:
        out = _norm(h, final_norm_w, seq=seq, delta=delta, gate=gate, out_dtype=F32)[1]
    return out.reshape(batch, seq, d)
```

```python
import functools
import math

import jax
import jax.numpy as jnp
from jax import lax
from jax.experimental import pallas as pl
from jax.experimental.pallas import tpu as pltpu

F32 = jnp.float32
BF16 = jnp.bfloat16
I32 = jnp.int32

LANE = 128
MXU_N = 256
V7X_VMEM_BYTES = 64 * 1024 * 1024
VMEM_LIMIT = V7X_VMEM_BYTES * 3 // 4

RMS_EPS = 1e-6
DA_HEADS = 8
ROPE_THETA = 500000.0
GDN_HEAD_DIM = 128
GDN_CHUNK = 64
L2_EPS = 1e-6
TOP_K = 2
NEG = -1e30


def _cparams(*sem):
    return pltpu.CompilerParams(dimension_semantics=sem, vmem_limit_bytes=VMEM_LIMIT)


def _silu(x):
    return x / (1.0 + jnp.exp(-x))


def _adaln_kernel(c_ref, w_ref, b_ref, o_ref):
    ca = _silu(c_ref[...]).astype(BF16)
    o_ref[0] = jnp.dot(ca, w_ref[0].astype(BF16), preferred_element_type=F32) + b_ref[0]


def _adaln(c, ada_w, ada_b):
    depth, d, n = ada_w.shape
    b = c.shape[0]
    tn = 1024
    return pl.pallas_call(
        _adaln_kernel,
        grid=(depth, n // tn),
        in_specs=[pl.BlockSpec((b, d), lambda l, j: (0, 0)),
                  pl.BlockSpec((1, d, tn), lambda l, j: (l, 0, j)),
                  pl.BlockSpec((1, 1, tn), lambda l, j: (l, 0, j))],
        out_specs=pl.BlockSpec((1, b, tn), lambda l, j: (l, 0, j)),
        out_shape=jax.ShapeDtypeStruct((depth, b, n), F32),
        compiler_params=_cparams("parallel", "parallel"), name="adaln",
    )(c, ada_w, ada_b.reshape(depth, 1, n))


def _norm_kernel(*refs, has_delta, has_mod, n_experts):
    refs = list(refs)
    h_ref = refs.pop(0)
    if has_delta:
        d_ref, g_ref = refs.pop(0), refs.pop(0)
    w_ref = refs.pop(0)
    if has_mod:
        sc_ref, sh_ref = refs.pop(0), refs.pop(0)
    if n_experts:
        r_ref = refs.pop(0)
    if has_delta:
        hn_ref = refs.pop(0)
    u_ref = refs.pop(0)
    if n_experts:
        gates_ref, sel_ref = refs.pop(0), refs.pop(0)

    x = h_ref[...]
    if has_delta:
        x = x + g_ref[0] * d_ref[...]
        hn_ref[...] = x
    y = x * lax.rsqrt(jnp.mean(x * x, axis=-1, keepdims=True) + RMS_EPS) * w_ref[...]
    if has_mod:
        y = y * (1.0 + sc_ref[0]) + sh_ref[0]
    u_ref[...] = y.astype(u_ref.dtype)
    if n_experts:
        logits = jnp.dot(y, r_ref[...], preferred_element_type=F32, precision=lax.Precision.HIGHEST)
        lane = lax.broadcasted_iota(I32, logits.shape, 1)
        logits = jnp.where(lane < n_experts, logits, -jnp.inf)
        m1 = jnp.max(logits, axis=1, keepdims=True)
        i1 = jnp.min(jnp.where(logits == m1, lane, LANE), axis=1, keepdims=True)
        oh1 = lane == i1
        rest = jnp.where(oh1, -jnp.inf, logits)
        m2 = jnp.max(rest, axis=1, keepdims=True)
        i2 = jnp.min(jnp.where(rest == m2, lane, LANE), axis=1, keepdims=True)
        oh2 = lane == i2
        e = jnp.exp(m2 - m1)
        gates_ref[...] = jnp.where(oh1, 1.0 / (1.0 + e), 0.0) + jnp.where(oh2, e / (1.0 + e), 0.0)
        sel_ref[...] = jnp.where(oh1 | oh2, 1.0, 0.0)


def _norm(h, w, *, seq, delta=None, gate=None, sc=None, sh=None, router=None, n_experts=0,
          out_dtype=BF16):
    t, d = h.shape
    tm = 256
    per_seq = seq // tm
    row = pl.BlockSpec((tm, d), lambda i: (i, 0))
    vec = pl.BlockSpec((1, d), lambda i: (0, 0))
    bvec = pl.BlockSpec((1, 1, d), lambda i: (i // per_seq, 0, 0))
    args, in_specs = [h], [row]
    if delta is not None:
        args += [delta, gate]
        in_specs += [row, bvec]
    args.append(w.reshape(1, d))
    in_specs.append(vec)
    if sc is not None:
        args += [sc, sh]
        in_specs += [bvec, bvec]
    if router is not None:
        args.append(router)
        in_specs.append(pl.BlockSpec((d, LANE), lambda i: (0, 0)))
    out_shape, out_specs = [], []
    if delta is not None:
        out_shape.append(jax.ShapeDtypeStruct((t, d), F32))
        out_specs.append(row)
    out_shape.append(jax.ShapeDtypeStruct((t, d), out_dtype))
    out_specs.append(row)
    if router is not None:
        out_shape += [jax.ShapeDtypeStruct((t, LANE), F32)] * 2
        out_specs += [pl.BlockSpec((tm, LANE), lambda i: (i, 0))] * 2
    kern = functools.partial(_norm_kernel, has_delta=delta is not None, has_mod=sc is not None,
                             n_experts=n_experts)
    return pl.pallas_call(kern, grid=(t // tm,), in_specs=in_specs, out_specs=out_specs,
                          out_shape=out_shape, compiler_params=_cparams("parallel"), name="norm")(*args)


def _rope_group(x, c, s1, s2, half):
    return x * c + pltpu.roll(x, LANE - half, axis=1) * s1 + pltpu.roll(x, half, axis=1) * s2


def _mm_kernel(*refs, mode, n_q, n_qk, q_scale, rope_half):
    if mode == "rope":
        a_ref, w_ref, c_ref, s1_ref, s2_ref, o_ref = refs
    elif mode == "resid":
        a_ref, w_ref, h_ref, g_ref, o_ref = refs
    else:
        a_ref, w_ref, o_ref = refs
    a = a_ref[...]
    groups = [slice(n * MXU_N, (n + 1) * MXU_N) for n in range(o_ref.shape[1] // MXU_N)]
    if mode == "resid":
        for cs in groups:
            acc = jnp.dot(a, w_ref[:, cs], preferred_element_type=F32)
            o_ref[:, cs] = h_ref[:, cs] + g_ref[0, :, cs] * acc
    elif mode == "rope":
        j = pl.program_id(1)

        @pl.when(j < n_qk)
        def _():
            c, s1, s2 = c_ref[...], s1_ref[...], s2_ref[...]
            scale = jnp.where(j < n_q, q_scale, 1.0)
            for cs in groups:
                acc = jnp.dot(a, w_ref[:, cs], preferred_element_type=F32)
                for g in range(MXU_N // LANE):
                    sl = slice(cs.start + g * LANE, cs.start + (g + 1) * LANE)
                    xg = acc[:, g * LANE:(g + 1) * LANE]
                    o_ref[:, sl] = (_rope_group(xg, c, s1, s2, rope_half) * scale).astype(o_ref.dtype)

        @pl.when(j >= n_qk)
        def _():
            for cs in groups:
                o_ref[:, cs] = jnp.dot(a, w_ref[:, cs], preferred_element_type=F32).astype(o_ref.dtype)
    else:
        for cs in groups:
            o_ref[:, cs] = jnp.dot(a, w_ref[:, cs], preferred_element_type=F32).astype(o_ref.dtype)


def _matmul(a, w, *, out_dtype=BF16, mode="plain", seq=None, h=None, gate=None, rope=None,
            rope_cols=(0, 0), rope_half=0, q_scale=1.0, tm=1024, tn=512):
    m, k = a.shape
    n = w.shape[1]
    tm, tn = min(tm, m, seq or m), min(tn, n)
    in_specs = [pl.BlockSpec((tm, k), lambda i, j: (i, 0)), pl.BlockSpec((k, tn), lambda i, j: (0, j))]
    args = [a, w]
    if mode == "rope":
        tab = pl.BlockSpec((tm, LANE), lambda i, j: (i, 0))
        in_specs += [tab, tab, tab]
        args += list(rope)
    elif mode == "resid":
        per_seq = seq // tm
        in_specs += [pl.BlockSpec((tm, tn), lambda i, j: (i, j)),
                     pl.BlockSpec((1, 1, tn), lambda i, j: (i // per_seq, 0, j))]
        args += [h, gate]
    kern = functools.partial(_mm_kernel, mode=mode, n_q=rope_cols[0] // tn, n_qk=rope_cols[1] // tn,
                             q_scale=q_scale, rope_half=rope_half)
    return pl.pallas_call(
        kern, grid=(m // tm, n // tn), in_specs=in_specs,
        out_specs=pl.BlockSpec((tm, tn), lambda i, j: (i, j)),
        out_shape=jax.ShapeDtypeStruct((m, n), out_dtype),
        compiler_params=_cparams("parallel", "parallel"), name="proj_" + mode,
    )(*args)


def _conv_silu(x, prev, w, kw):
    head = jnp.concatenate([prev, x[:8]], axis=0)
    y = x * w[kw - 1:kw]
    y_head = head * w[kw - 1:kw]
    for s in range(1, kw):
        y = y + pltpu.roll(x, s, axis=0) * w[kw - 1 - s:kw - s]
        y_head = y_head + pltpu.roll(head, s, axis=0) * w[kw - 1 - s:kw - s]
    return _silu(jnp.concatenate([y_head[8:], y[8:]], axis=0))


def _gdn_proj_kernel(a_ref, w_ref, cw_ref, o_ref, carry_ref, *, n_q, n_qk, n_conv, per_seq, kw):
    i, j = pl.program_id(0), pl.program_id(1)
    tm = a_ref.shape[0]
    a = a_ref[...]
    groups = [slice(n * MXU_N, (n + 1) * MXU_N) for n in range(o_ref.shape[1] // MXU_N)]

    @pl.when(i == 0)
    def _():
        carry_ref[j] = jnp.zeros(carry_ref.shape[1:], F32)

    def conv_tile(kind):
        first = i % per_seq == 0
        for cs in groups:
            acc = jnp.dot(a, w_ref[:, cs], preferred_element_type=F32)
            prev = jnp.where(first, 0.0, carry_ref[j, :, cs])
            carry_ref[j, :, cs] = acc[tm - 8:]
            y = _conv_silu(acc, prev, cw_ref[:, cs], kw)
            if kind == "v":
                o_ref[:, cs] = y.astype(o_ref.dtype)
            else:
                scale = GDN_HEAD_DIM ** -0.5 if kind == "q" else 1.0
                for g in range(MXU_N // GDN_HEAD_DIM):
                    yg = y[:, g * GDN_HEAD_DIM:(g + 1) * GDN_HEAD_DIM]
                    yn = yg * lax.rsqrt(jnp.sum(yg * yg, axis=-1, keepdims=True) + L2_EPS)
                    sl = slice(cs.start + g * GDN_HEAD_DIM, cs.start + (g + 1) * GDN_HEAD_DIM)
                    o_ref[:, sl] = (yn * scale).astype(o_ref.dtype)

    pl.when(j < n_q)(functools.partial(conv_tile, "q"))
    pl.when((j >= n_q) & (j < n_qk))(functools.partial(conv_tile, "k"))
    pl.when((j >= n_qk) & (j < n_conv))(functools.partial(conv_tile, "v"))

    @pl.when(j >= n_conv)
    def _():
        for cs in groups:
            o_ref[:, cs] = jnp.dot(a, w_ref[:, cs], preferred_element_type=F32).astype(o_ref.dtype)


def _gdn_proj(a, w, conv_w, *, key_dim, seq, tm=1024, tn=1024):
    m, k = a.shape
    n = w.shape[1]
    kw, conv_dim = conv_w.shape
    tm, tn = min(tm, m, seq), min(tn, n)
    n_conv = conv_dim // tn
    kern = functools.partial(_gdn_proj_kernel, n_q=key_dim // tn, n_qk=2 * key_dim // tn, n_conv=n_conv,
                             per_seq=seq // tm, kw=kw)
    return pl.pallas_call(
        kern, grid=(m // tm, n // tn),
        in_specs=[pl.BlockSpec((tm, k), lambda i, j: (i, 0)),
                  pl.BlockSpec((k, tn), lambda i, j: (0, j)),
                  pl.BlockSpec((kw, tn), lambda i, j: (0, jnp.minimum(j, n_conv - 1)))],
        out_specs=pl.BlockSpec((tm, tn), lambda i, j: (i, j)),
        out_shape=jax.ShapeDtypeStruct((m, n), BF16),
        scratch_shapes=[pltpu.VMEM((n // tn, 8, tn), F32)],
        compiler_params=_cparams("arbitrary", "arbitrary"), name="gdn_proj",
    )(a, w, conv_w)


def _attn_kernel(lam_ref, q_ref, k_ref, v_ref, sub_ref, o_ref, m_ref, l_ref, acc_ref, alpha_ref, p_ref, *,
                 blk, rq, rc, dh, lambda_init):
    qi, ki = pl.program_id(2), pl.program_id(3)

    @pl.when(ki == 0)
    def _():
        m_ref[...] = jnp.full(m_ref.shape, NEG, F32)
        l_ref[...] = jnp.zeros(l_ref.shape, F32)
        acc_ref[...] = jnp.zeros(acc_ref.shape, F32)

    def step(n_keys, diag_key0):
        v = v_ref[:n_keys]
        n_t = n_keys // LANE
        if diag_key0 is not None:
            row = lax.broadcasted_iota(I32, (rc, LANE), 0)
            col = lax.broadcasted_iota(I32, (rc, LANE), 1)
        for q0 in range(0, blk, rq):
            for j in range(2):
                sl = slice(j * dh, (j + 1) * dh)
                s = lax.dot_general(q_ref[q0:q0 + rq, sl], k_ref[:n_keys, sl], (((1,), (1,)), ((), ())),
                                    preferred_element_type=F32)
                for r0 in range(0, rq, rc):
                    rs = slice(q0 + r0, q0 + r0 + rc)
                    tiles = [s[r0:r0 + rc, n * LANE:(n + 1) * LANE] for n in range(n_t)]
                    if diag_key0 is not None:
                        tiles = [tl if n * LANE < diag_key0 else
                                 jnp.where(col + (n * LANE - diag_key0) <= row + (q0 + r0), tl, NEG)
                                 for n, tl in enumerate(tiles)]
                    mx = tiles[0]
                    for tl in tiles[1:]:
                        mx = jnp.maximum(mx, tl)
                    m_prev = m_ref[j, rs]
                    m_new = jnp.maximum(m_prev, jnp.max(mx, axis=1, keepdims=True))
                    alpha = jnp.exp2(m_prev - m_new)
                    ps = [jnp.exp2(tl - m_new) for tl in tiles]
                    psum = ps[0]
                    for pt in ps[1:]:
                        psum = psum + pt
                    l_ref[j, rs] = alpha * l_ref[j, rs] + psum
                    m_ref[j, rs] = m_new
                    alpha_ref[j, rs] = alpha
                    p_ref[j, rs, :n_keys] = jnp.concatenate(ps, axis=1).astype(BF16)
                qs = slice(q0, q0 + rq)
                pv = jnp.dot(p_ref[j, qs, :n_keys], v, preferred_element_type=F32)
                for n in range(pv.shape[1] // LANE):
                    vl = slice(n * LANE, (n + 1) * LANE)
                    acc_ref[j, qs, vl] = alpha_ref[j, qs] * acc_ref[j, qs, vl] + pv[:, vl]

    last = qi // 2

    @pl.when(ki < last)
    def _():
        step(2 * blk, None)

    @pl.when((ki == last) & (qi % 2 == 1))
    def _():
        step(2 * blk, blk)

    @pl.when((ki == last) & (qi % 2 == 0))
    def _():
        step(blk, 0)

    @pl.when(ki == pl.num_programs(3) - 1)
    def _():
        lv = lam_ref[...]
        lam = (jnp.exp(jnp.sum(lv[0:1] * lv[1:2], axis=1, keepdims=True))
               - jnp.exp(jnp.sum(lv[2:3] * lv[3:4], axis=1, keepdims=True)) + lambda_init)
        l0 = jnp.sum(l_ref[0], axis=1, keepdims=True)
        l1 = jnp.sum(l_ref[1], axis=1, keepdims=True)
        o = acc_ref[0] / l0 - lam * (acc_ref[1] / l1)
        o = o * lax.rsqrt(jnp.mean(o * o, axis=-1, keepdims=True) + RMS_EPS) * sub_ref[...]
        o_ref[...] = (o * (1.0 - lambda_init)).astype(o_ref.dtype)


def _diff_attention(qkv, lam_vec, subln, *, batch, seq, layer_idx):
    t, d3 = qkv.shape
    d = d3 // 3
    dv = d // DA_HEADS
    dh = dv // 2
    blk = 512
    nb = seq // blk
    lambda_init = 0.8 - 0.6 * math.exp(-0.3 * layer_idx)
    kern = functools.partial(_attn_kernel, blk=blk, rq=256, rc=64, dh=dh, lambda_init=lambda_init)
    nkb = nb // 2
    kv_row = lambda b, qi, ki: b * nkb + jnp.minimum(ki, qi // 2)
    return pl.pallas_call(
        kern, grid=(batch, DA_HEADS, nb, nkb),
        in_specs=[pl.BlockSpec((4, dh), lambda b, h, qi, ki: (0, 0)),
                  pl.BlockSpec((blk, dv), lambda b, h, qi, ki: (b * nb + qi, h)),
                  pl.BlockSpec((2 * blk, dv), lambda b, h, qi, ki: (kv_row(b, qi, ki), DA_HEADS + h)),
                  pl.BlockSpec((2 * blk, dv), lambda b, h, qi, ki: (kv_row(b, qi, ki), 2 * DA_HEADS + h)),
                  pl.BlockSpec((1, dv), lambda b, h, qi, ki: (0, 0))],
        out_specs=pl.BlockSpec((blk, dv), lambda b, h, qi, ki: (b * nb + qi, h)),
        out_shape=jax.ShapeDtypeStruct((t, d), BF16),
        scratch_shapes=[pltpu.VMEM((2, blk, LANE), F32), pltpu.VMEM((2, blk, LANE), F32),
                        pltpu.VMEM((2, blk, dv), F32), pltpu.VMEM((2, blk, LANE), F32),
                        pltpu.VMEM((2, blk, 2 * blk), BF16)],
        compiler_params=_cparams("parallel", "parallel", "parallel", "arbitrary"), name="diff_attn",
    )(lam_vec, qkv, qkv, qkv, subln.reshape(1, dv))


def _split3(x):
    hi = x.astype(BF16)
    r = x - hi.astype(F32)
    mid = r.astype(BF16)
    lo = (r - mid.astype(F32)).astype(BF16)
    return hi, mid, lo


def _gates_kernel(ba_ref, alog_ref, dt_ref, beta_ref, gc_ref):
    x = ba_ref[...]
    tm = x.shape[0]
    b_raw, a_raw = x[:, :LANE], x[:, LANE:]
    beta_ref[...] = 1.0 / (1.0 + jnp.exp(-b_raw))
    z = a_raw + dt_ref[...]
    softplus = jnp.maximum(z, 0.0) + jnp.log(1.0 + jnp.exp(-jnp.abs(z)))
    g = -jnp.exp(alog_ref[...]) * softplus
    row = lax.broadcasted_iota(I32, (tm, tm), 0)
    col = lax.broadcasted_iota(I32, (tm, tm), 1)
    tri = jnp.where((row >= col) & (row // GDN_CHUNK == col // GDN_CHUNK), 1.0, 0.0).astype(BF16)
    gc_ref[...] = sum(jnp.dot(tri, part, preferred_element_type=F32) for part in _split3(g))


def _gdn_gates(ba, a_log, dt_bias):
    t = ba.shape[0]
    hv = a_log.shape[0]
    tm = 256
    pad = lambda v: jnp.pad(v.astype(F32), (0, LANE - hv)).reshape(1, LANE)
    blk = pl.BlockSpec((tm, LANE), lambda i: (i, 0))
    vec = pl.BlockSpec((1, LANE), lambda i: (0, 0))
    return pl.pallas_call(
        _gates_kernel, grid=(t // tm,),
        in_specs=[pl.BlockSpec((tm, 2 * LANE), lambda i: (i, 0)), vec, vec],
        out_specs=[blk, blk],
        out_shape=[jax.ShapeDtypeStruct((t, LANE), F32)] * 2,
        compiler_params=_cparams("parallel"), name="gdn_gates",
    )(ba, pad(a_log), pad(dt_bias))


def _delta_kernel(q_ref, k_ref, v_ref, z_ref, beta_ref, gc_ref, gct_ref, nw_ref, o_ref, state_ref, *,
                  heads, rows):
    hg, blk = pl.program_id(1), pl.program_id(2)
    dk = GDN_HEAD_DIM
    c = GDN_CHUNK
    n_chunks = rows // c

    @pl.when(blk == 0)
    def _():
        state_ref[...] = jnp.zeros(state_ref.shape, F32)

    n_pairs = rows // LANE
    ri = lax.broadcasted_iota(I32, (c, LANE), 0)
    li = lax.broadcasted_iota(I32, (c, LANE), 1)
    lj = li & (c - 1)
    left = li < c
    strict = ri > lj
    causal = ri >= lj
    same16 = (ri >> 4) == (lj >> 4)
    same32 = (ri >> 5) == (lj >> 5)
    m16 = strict & same16
    m32 = strict & same32 & jnp.logical_not(same16)
    m64 = strict & jnp.logical_not(same32)
    eye = jnp.where(ri == lj, 1.0, 0.0)
    bd_mask = ((lax.broadcasted_iota(I32, (LANE, LANE), 0) < c)
               == (lax.broadcasted_iota(I32, (LANE, LANE), 1) < c))
    lane = lax.broadcasted_iota(I32, (rows, LANE), 1)
    contract_last = (((1,), (1,)), ((), ()))
    contract_first = (((0,), (0,)), ((), ()))

    def bd(p):
        return jnp.where(bd_mask, jnp.concatenate([p, p], axis=0), 0.0).astype(BF16)

    def pm(p, q):
        return jnp.dot(p.astype(BF16), bd(q), preferred_element_type=F32)

    def pack(full):
        return jnp.where(left, full[:c], full[c:])

    gc_all, beta_all = gc_ref[...], beta_ref[...]
    eg_all = jnp.exp(gc_all)
    pairs = [(g, p) for g in range(heads) for p in range(n_pairs)]
    prow = lambda x, p: x[p * LANE:(p + 1) * LANE]

    kkt, qkt = {}, {}
    for kh in range(heads // 2):
        ks = slice(kh * dk, (kh + 1) * dk)
        for p in range(n_pairs):
            kp = k_ref[p * LANE:(p + 1) * LANE, ks]
            qp = q_ref[p * LANE:(p + 1) * LANE, ks]
            kkt[kh, p] = pack(lax.dot_general(kp, kp, contract_last, preferred_element_type=F32))
            qkt[kh, p] = pack(lax.dot_general(qp, kp, contract_last, preferred_element_type=F32))

    hd = []
    for g in range(heads):
        pick = lane == hg * heads + g
        col = lambda x: jnp.sum(jnp.where(pick, x, 0.0), axis=1, keepdims=True)
        gc_col, beta_col, eg_col = col(gc_all), col(beta_all), col(eg_all)
        ks = slice((g // 2) * dk, (g // 2 + 1) * dk)
        kf = k_ref[:, ks].astype(F32)
        k_beta = kf * beta_col
        y = jnp.concatenate([v_ref[:, g * dk:(g + 1) * dk].astype(F32) * beta_col, k_beta * eg_col],
                            axis=1).astype(BF16)
        gl = [gc_col[ci * c + c - 1:ci * c + c] for ci in range(n_chunks)]
        gl_col = jnp.concatenate([jnp.broadcast_to(v, (c, 1)) for v in gl], axis=0)
        hd.append(dict(gc=gc_col, beta=beta_col, y=y,
                       q_dec=(q_ref[:, ks].astype(F32) * eg_col).astype(BF16),
                       k_dec=(kf * jnp.exp(gl_col - gc_col)).astype(BF16),
                       g_end=[jnp.exp(v) for v in gl]))

    a, intra = {}, {}
    for g, p in pairs:
        gcp = jnp.where(left, prow(hd[g]["gc"], p)[:c], prow(hd[g]["gc"], p)[c:])
        bp = jnp.where(left, prow(hd[g]["beta"], p)[:c], prow(hd[g]["beta"], p)[c:])
        decay = jnp.exp(jnp.where(causal, gcp - gct_ref[g][:, p * LANE:(p + 1) * LANE], NEG))
        a[g, p] = kkt[g // 2, p] * bp * decay
        intra[g, p] = qkt[g // 2, p] * decay

    d1 = {n: jnp.where(m16, a[n], 0.0) for n in pairs}
    d2 = {n: pm(d1[n], d1[n]) for n in pairs}
    d4 = {n: pm(d2[n], d2[n]) for n in pairs}
    x = {n: eye - d1[n] for n in pairs}
    x = {n: x[n] + pm(x[n], d2[n]) for n in pairs}
    d8 = {n: pm(d4[n], d4[n]) for n in pairs}
    x = {n: x[n] + pm(x[n], d4[n]) for n in pairs}
    x = {n: x[n] + pm(x[n], d8[n]) for n in pairs}
    for mask in (m32, m64):
        t = {n: pm(jnp.where(mask, a[n], 0.0), x[n]) for n in pairs}
        x = {n: x[n] - pm(x[n], t[n]) for n in pairs}
    uw = {(g, p): jnp.dot(bd(x[g, p]), prow(hd[g]["y"], p), preferred_element_type=F32) for g, p in pairs}

    s = [state_ref[g] for g in range(heads)]
    v_new = {}
    o_state = {}
    for ci in range(n_chunks):
        p, half = divmod(ci, 2)
        hs = slice(half * c, (half + 1) * c)
        rs = slice(ci * c, (ci + 1) * c)
        sb = [s[g].astype(BF16) for g in range(heads)]
        for g in range(heads):
            u_c, w_c = uw[g, p][hs, :dk], uw[g, p][hs, dk:]
            v_new[g, ci] = u_c - jnp.dot(w_c.astype(BF16), sb[g], preferred_element_type=F32)
            o_state[g, ci] = jnp.dot(hd[g]["q_dec"][rs], sb[g], preferred_element_type=F32)
        for g in range(heads):
            s[g] = s[g] * hd[g]["g_end"][ci] + lax.dot_general(
                hd[g]["k_dec"][rs], v_new[g, ci].astype(BF16), contract_first, preferred_element_type=F32)
    for g in range(heads):
        state_ref[g] = s[g]

    for g in range(heads):
        o_pairs = []
        for p in range(n_pairs):
            vn = jnp.concatenate([v_new[g, 2 * p], v_new[g, 2 * p + 1]], axis=0).astype(BF16)
            o_in = jnp.dot(bd(intra[g, p]), vn, preferred_element_type=F32)
            o_pairs.append(o_in + jnp.concatenate([o_state[g, 2 * p], o_state[g, 2 * p + 1]], axis=0))
        o = jnp.concatenate(o_pairs, axis=0)
        o = o * lax.rsqrt(jnp.mean(o * o, axis=-1, keepdims=True) + RMS_EPS) * nw_ref[...]
        vs = slice(g * dk, (g + 1) * dk)
        o_ref[:, vs] = (o * _silu(z_ref[:, vs].astype(F32))).astype(o_ref.dtype)


def _delta_rule(proj, key_dim, val_dim, beta, gc, gct, norm_w, *, batch, seq):
    t = proj.shape[0]
    vdim = val_dim
    dk = GDN_HEAD_DIM
    hv = vdim // dk
    heads = 8
    rows = 256
    nblk = seq // rows
    qw, vw = heads // 2 * dk, heads * dk
    kb, vb, zb = key_dim // qw, 2 * key_dim // vw, (2 * key_dim + val_dim) // vw
    kern = functools.partial(_delta_kernel, heads=heads, rows=rows)
    tok = lambda b, hg, i: (b * nblk + i, 0)
    return pl.pallas_call(
        kern, grid=(batch, hv // heads, nblk),
        in_specs=[pl.BlockSpec((rows, qw), lambda b, hg, i: (b * nblk + i, hg)),
                  pl.BlockSpec((rows, qw), lambda b, hg, i: (b * nblk + i, kb + hg)),
                  pl.BlockSpec((rows, vw), lambda b, hg, i: (b * nblk + i, vb + hg)),
                  pl.BlockSpec((rows, vw), lambda b, hg, i: (b * nblk + i, zb + hg)),
                  pl.BlockSpec((rows, LANE), tok),
                  pl.BlockSpec((rows, LANE), tok),
                  pl.BlockSpec((heads, 1, rows), lambda b, hg, i: (hg, 0, b * nblk + i)),
                  pl.BlockSpec((1, dk), lambda b, hg, i: (0, 0))],
        out_specs=pl.BlockSpec((rows, vw), lambda b, hg, i: (b * nblk + i, hg)),
        out_shape=jax.ShapeDtypeStruct((t, vdim), BF16),
        scratch_shapes=[pltpu.VMEM((heads, dk, dk), F32)],
        compiler_params=_cparams("parallel", "parallel", "arbitrary"), name="delta_rule",
    )(proj, proj, proj, proj, beta, gc, gct, norm_w.reshape(1, dk))


def _ffn_kernel(te_ref, *refs, mode):
    del te_ref
    if mode == "resid":
        x_ref, wg_ref, wu_ref, wd_ref, h_ref, g_ref, o_ref, acc_ref = refs
    else:
        x_ref, wg_ref, wu_ref, wd_ref, o_ref, acc_ref = refs
    f = pl.program_id(1)

    @pl.when(f == 0)
    def _():
        acc_ref[...] = jnp.zeros(acc_ref.shape, F32)

    x = x_ref[...]
    tf = wg_ref.shape[2]
    part = None
    for n in range(tf // MXU_N):
        cs = slice(n * MXU_N, (n + 1) * MXU_N)
        gate = jnp.dot(x, wg_ref[0, :, cs], preferred_element_type=F32)
        up = jnp.dot(x, wu_ref[0, :, cs], preferred_element_type=F32)
        dn = jnp.dot((_silu(gate) * up).astype(BF16), wd_ref[0, cs, :], preferred_element_type=F32)
        part = dn if part is None else part + dn
    acc_ref[...] += part

    @pl.when(f == pl.num_programs(1) - 1)
    def _():
        if mode == "resid":
            o_ref[...] = h_ref[...] + g_ref[0] * acc_ref[...]
        else:
            o_ref[...] = acc_ref[...].astype(o_ref.dtype)


def _ffn(x, wg, wu, wd, tile_expert, *, tm, mode, seq=None, h=None, gate=None):
    m, d = x.shape
    ff = wg.shape[2]
    tf = 512
    xs = pl.BlockSpec((tm, d), lambda i, f, te: (i, 0))
    in_specs = [xs,
                pl.BlockSpec((1, d, tf), lambda i, f, te: (te[i], 0, f)),
                pl.BlockSpec((1, d, tf), lambda i, f, te: (te[i], 0, f)),
                pl.BlockSpec((1, tf, d), lambda i, f, te: (te[i], f, 0))]
    args = [x, wg, wu, wd]
    if mode == "resid":
        per_seq = seq // tm
        in_specs += [xs, pl.BlockSpec((1, 1, d), lambda i, f, te: (i // per_seq, 0, 0))]
        args += [h, gate]
        out_dtype = F32
    else:
        out_dtype = BF16
    return pl.pallas_call(
        functools.partial(_ffn_kernel, mode=mode),
        grid_spec=pltpu.PrefetchScalarGridSpec(
            num_scalar_prefetch=1, grid=(m // tm, ff // tf), in_specs=in_specs, out_specs=xs,
            scratch_shapes=[pltpu.VMEM((tm, d), F32)]),
        out_shape=jax.ShapeDtypeStruct((m, d), out_dtype),
        compiler_params=_cparams("parallel", "arbitrary"), name="swiglu_" + mode,
    )(tile_expert, *args)


def _scatter_kernel(sa_ref, sb_ref, x_ref, init_ref, o_ref, sem, *, tm):
    del init_ref
    def issue(r, carry):
        pltpu.make_async_copy(x_ref.at[r], o_ref.at[sa_ref[r]], sem).start()
        pltpu.make_async_copy(x_ref.at[r], o_ref.at[sb_ref[r]], sem).start()
        return carry
    lax.fori_loop(0, tm, issue, 0)
    drain = pltpu.make_async_copy(x_ref, o_ref.at[pl.ds(0, tm)], sem)
    drain.wait()
    drain.wait()


def _scatter_rows(x, slot_a, slot_b, n_slots, *, tm):
    t, s, l = x.shape
    ispec = pl.BlockSpec((tm,), lambda i: (i,), memory_space=pltpu.SMEM)
    return pl.pallas_call(
        functools.partial(_scatter_kernel, tm=tm), grid=(t // tm,),
        in_specs=[ispec, ispec, pl.BlockSpec((tm, s, l), lambda i: (i, 0, 0)), pl.BlockSpec(memory_space=pl.ANY)],
        out_specs=pl.BlockSpec(memory_space=pl.ANY),
        out_shape=jax.ShapeDtypeStruct((n_slots, s, l), x.dtype),
        scratch_shapes=[pltpu.SemaphoreType.DMA(())],
        input_output_aliases={3: 0},
        compiler_params=_cparams("arbitrary"), name="scatter_rows",
    )(slot_a, slot_b, x, jnp.zeros((n_slots, s, l), x.dtype))


def _combine_kernel(ia_ref, ib_ref, wa_ref, wb_ref, src_ref, o_ref, a_buf, b_buf, sem_a, sem_b, *, tm):
    def issue(r, carry):
        pltpu.make_async_copy(src_ref.at[ia_ref[r]], a_buf.at[r], sem_a).start()
        pltpu.make_async_copy(src_ref.at[ib_ref[r]], b_buf.at[r], sem_b).start()
        return carry
    lax.fori_loop(0, tm, issue, 0)
    pltpu.make_async_copy(src_ref.at[pl.ds(0, tm)], a_buf, sem_a).wait()
    pltpu.make_async_copy(src_ref.at[pl.ds(0, tm)], b_buf, sem_b).wait()
    o_ref[...] = wa_ref[...] * a_buf[...].astype(F32) + wb_ref[...] * b_buf[...].astype(F32)


def _combine_rows(src, idx_a, idx_b, w_a, w_b, *, tm):
    _, s, l = src.shape
    t = idx_a.shape[0]
    ispec = pl.BlockSpec((tm,), lambda i: (i,), memory_space=pltpu.SMEM)
    wspec = pl.BlockSpec((tm, 1, l), lambda i: (i, 0, 0))
    return pl.pallas_call(
        functools.partial(_combine_kernel, tm=tm), grid=(t // tm,),
        in_specs=[ispec, ispec, wspec, wspec, pl.BlockSpec(memory_space=pl.ANY)],
        out_specs=pl.BlockSpec((tm, s, l), lambda i: (i, 0, 0)),
        out_shape=jax.ShapeDtypeStruct((t, s, l), F32),
        scratch_shapes=[pltpu.VMEM((tm, s, l), src.dtype), pltpu.VMEM((tm, s, l), src.dtype),
                        pltpu.SemaphoreType.DMA(()), pltpu.SemaphoreType.DMA(())],
        compiler_params=_cparams("arbitrary"), name="combine_rows",
    )(idx_a, idx_b, w_a, w_b, src)


def _dispatch_plan(sel, gates, n_experts, tm):
    t = sel.shape[0]
    chosen = sel[:, :n_experts] > 0.0
    seli = chosen.astype(I32)
    counts = jnp.sum(seli, axis=0)
    padded = (counts + tm - 1) // tm * tm
    ends = jnp.cumsum(padded)
    starts = ends - padded
    rank = jnp.cumsum(seli, axis=0) - seli
    slot = starts[None, :] + rank
    n_slots = t * TOP_K + n_experts * tm
    slot_a = jnp.min(jnp.where(chosen, slot, n_slots), axis=1).astype(I32)
    slot_b = jnp.max(jnp.where(chosen, slot, -1), axis=1).astype(I32)
    g = gates[:, :n_experts]
    lanes = lambda w: jnp.broadcast_to(w[:, None, None], (t, 1, LANE))
    w_a = lanes(jnp.sum(jnp.where(chosen & (slot == slot_a[:, None]), g, 0.0), axis=1))
    w_b = lanes(jnp.sum(jnp.where(chosen & (slot == slot_b[:, None]), g, 0.0), axis=1))
    tile_start = jnp.arange(n_slots // tm, dtype=I32) * tm
    tile_expert = jnp.minimum(jnp.searchsorted(ends, tile_start, side="right"), n_experts - 1).astype(I32)
    return slot_a, slot_b, w_a, w_b, tile_expert, n_slots


def _rope_tables(positions, rot_dim):
    half = rot_dim // 2
    inv_freq = 1.0 / (ROPE_THETA ** (jnp.arange(0, rot_dim, 2, dtype=F32) / rot_dim))
    ang = positions.astype(F32).reshape(-1, 1) * inv_freq
    cos, sin = jnp.cos(ang), jnp.sin(ang)
    t = ang.shape[0]
    ones = jnp.ones((t, LANE - rot_dim), F32)
    zeros = jnp.zeros((t, LANE - half), F32)
    c = jnp.concatenate([cos, cos, ones], axis=1)
    s1 = jnp.concatenate([-sin, zeros], axis=1)
    s2 = jnp.concatenate([zeros[:, :half], sin, zeros[:, :LANE - rot_dim]], axis=1)
    return c, s1, s2


def kernel(x, c, positions, ada_w, ada_b, norm1_w, norm2_w, attn_w_in, attn_lambda, attn_subln, attn_w_out, gdn_w_in, gdn_conv_w, gdn_a_log, gdn_dt_bias, gdn_norm_w, gdn_w_out, ffn_w_gate, ffn_w_up, ffn_w_down, moe_router, moe_w_gate, moe_w_up, moe_w_down, final_norm_w):
    batch, seq, d = x.shape
    t = batch * seq
    depth = ada_w.shape[0]
    n_experts = moe_router.shape[-1]
    dh = d // DA_HEADS // 2
    key_dim = d
    val_dim = gdn_w_out.shape[1]
    conv_dim = gdn_conv_w.shape[-1]
    hv = gdn_a_log.shape[-1]
    moe_tm = 512
    s_sub = d // LANE

    mod = _adaln(c, ada_w, ada_b)
    rope = _rope_tables(positions, dh // 4)
    h = x.reshape(t, d)
    delta = gate = None
    ffn_w = [w.astype(BF16) for w in (ffn_w_gate, ffn_w_up, ffn_w_down)]
    moe_w = [w.astype(BF16).reshape((-1,) + w.shape[2:]) for w in (moe_w_gate, moe_w_up, moe_w_down)]

    for i in range(depth):
        j = i // 2
        sh1, sc1, g1, sh2, sc2, g2 = [mod[i, :, n * d:(n + 1) * d].reshape(batch, 1, d) for n in range(6)]
        if delta is None:
            u = _norm(h, norm1_w[i], seq=seq, sc=sc1, sh=sh1)[0]
        else:
            h, u = _norm(h, norm1_w[i], seq=seq, delta=delta, gate=gate, sc=sc1, sh=sh1)
            delta = gate = None
        if i % 2 == 0:
            qkv = _matmul(u, attn_w_in[j].astype(BF16), mode="rope", rope=rope, rope_cols=(d, 2 * d),
                          rope_half=dh // 8, q_scale=dh ** -0.5 * math.log2(math.e), tn=1024)
            o = _diff_attention(qkv, attn_lambda[j], attn_subln[j], batch=batch, seq=seq, layer_idx=i)
            h = _matmul(o, attn_w_out[j].astype(BF16), mode="resid", seq=seq, h=h, gate=g1, out_dtype=F32)
        else:
            w_in = gdn_w_in[j]
            n_main = conv_dim + val_dim
            proj = _gdn_proj(u, w_in[:, :n_main].astype(BF16), gdn_conv_w[j], key_dim=key_dim, seq=seq)
            w_ba = jnp.zeros((d, 2 * LANE), F32)
            w_ba = w_ba.at[:, :hv].set(w_in[:, n_main:n_main + hv]).at[:, LANE:LANE + hv].set(w_in[:, n_main + hv:])
            ba = _matmul(u, w_ba.astype(BF16), out_dtype=F32, tn=2 * LANE)
            beta, gc = _gdn_gates(ba, gdn_a_log[j], gdn_dt_bias[j])
            gct = gc[:, :hv].T.reshape(hv, 1, t)
            o = _delta_rule(proj, key_dim, val_dim, beta, gc, gct, gdn_norm_w[j], batch=batch, seq=seq)
            h = _matmul(o, gdn_w_out[j].astype(BF16), mode="resid", seq=seq, h=h, gate=g1, out_dtype=F32)
        if i % 2 == 0:
            u = _norm(h, norm2_w[i], seq=seq, sc=sc2, sh=sh2)[0]
            h = _ffn(u, *ffn_w, jnp.full((t // moe_tm,), j, I32), tm=moe_tm, mode="resid", seq=seq, h=h, gate=g2)
        else:
            router = jnp.pad(moe_router[j], ((0, 0), (0, LANE - n_experts)))
            u, gates, sel = _norm(h, norm2_w[i], seq=seq, sc=sc2, sh=sh2, router=router, n_experts=n_experts)
            slot_a, slot_b, w_a, w_b, tile_expert, n_slots = _dispatch_plan(sel, gates, n_experts, moe_tm)
            xs = _scatter_rows(u.reshape(t, s_sub, LANE), slot_a, slot_b, n_slots, tm=moe_tm)
            ys = _ffn(xs.reshape(-1, d), *moe_w, tile_expert + j * n_experts, tm=moe_tm, mode="plain")
            delta = _combine_rows(ys.reshape(-1, s_sub, LANE), slot_a, slot_b, w_a, w_b, tm=moe_tm).reshape(t, d)
            gate = g2
    if delta is None:
        out = _norm(h, final_norm_w, seq=seq, out_dtype=F32)[0]
    else:
        out = _norm(h, final_norm_w, seq=seq, delta=delta, gate=gate, out_dtype=F32)[1]
    return out.reshape(batch, seq, d)
```

```python
import functools
import math

import jax
import jax.numpy as jnp
from jax import lax
from jax.experimental import pallas as pl
from jax.experimental.pallas import tpu as pltpu

F32 = jnp.float32
BF16 = jnp.bfloat16
I32 = jnp.int32

LANE = 128
MXU_N = 256
V7X_VMEM_BYTES = 64 * 1024 * 1024
VMEM_LIMIT = V7X_VMEM_BYTES * 3 // 4

RMS_EPS = 1e-6
DA_HEADS = 8
ROPE_THETA = 500000.0
GDN_HEAD_DIM = 128
GDN_CHUNK = 64
L2_EPS = 1e-6
TOP_K = 2
NEG = -1e30


def _cparams(*sem):
    return pltpu.CompilerParams(dimension_semantics=sem, vmem_limit_bytes=VMEM_LIMIT)


def _silu(x):
    return x / (1.0 + jnp.exp(-x))


def _adaln_kernel(c_ref, w_ref, b_ref, o_ref):
    ca = _silu(c_ref[...]).astype(BF16)
    o_ref[0] = jnp.dot(ca, w_ref[0].astype(BF16), preferred_element_type=F32) + b_ref[0]


def _adaln(c, ada_w, ada_b):
    depth, d, n = ada_w.shape
    b = c.shape[0]
    tn = 1024
    return pl.pallas_call(
        _adaln_kernel,
        grid=(depth, n // tn),
        in_specs=[pl.BlockSpec((b, d), lambda l, j: (0, 0)),
                  pl.BlockSpec((1, d, tn), lambda l, j: (l, 0, j)),
                  pl.BlockSpec((1, 1, tn), lambda l, j: (l, 0, j))],
        out_specs=pl.BlockSpec((1, b, tn), lambda l, j: (l, 0, j)),
        out_shape=jax.ShapeDtypeStruct((depth, b, n), F32),
        compiler_params=_cparams("parallel", "parallel"), name="adaln",
    )(c, ada_w, ada_b.reshape(depth, 1, n))


def _norm_kernel(*refs, has_delta, has_mod, n_experts):
    refs = list(refs)
    h_ref = refs.pop(0)
    if has_delta:
        d_ref, g_ref = refs.pop(0), refs.pop(0)
    w_ref = refs.pop(0)
    if has_mod:
        sc_ref, sh_ref = refs.pop(0), refs.pop(0)
    if n_experts:
        r_ref = refs.pop(0)
    if has_delta:
        hn_ref = refs.pop(0)
    u_ref = refs.pop(0)
    if n_experts:
        gates_ref, sel_ref = refs.pop(0), refs.pop(0)

    x = h_ref[...]
    if has_delta:
        x = x + g_ref[0] * d_ref[...]
        hn_ref[...] = x
    y = x * lax.rsqrt(jnp.mean(x * x, axis=-1, keepdims=True) + RMS_EPS) * w_ref[...]
    if has_mod:
        y = y * (1.0 + sc_ref[0]) + sh_ref[0]
    u_ref[...] = y.astype(u_ref.dtype)
    if n_experts:
        logits = jnp.dot(y, r_ref[...], preferred_element_type=F32, precision=lax.Precision.HIGHEST)
        lane = lax.broadcasted_iota(I32, logits.shape, 1)
        logits = jnp.where(lane < n_experts, logits, -jnp.inf)
        m1 = jnp.max(logits, axis=1, keepdims=True)
        i1 = jnp.min(jnp.where(logits == m1, lane, LANE), axis=1, keepdims=True)
        oh1 = lane == i1
        rest = jnp.where(oh1, -jnp.inf, logits)
        m2 = jnp.max(rest, axis=1, keepdims=True)
        i2 = jnp.min(jnp.where(rest == m2, lane, LANE), axis=1, keepdims=True)
        oh2 = lane == i2
        e = jnp.exp(m2 - m1)
        gates_ref[...] = jnp.where(oh1, 1.0 / (1.0 + e), 0.0) + jnp.where(oh2, e / (1.0 + e), 0.0)
        sel_ref[...] = jnp.where(oh1 | oh2, 1.0, 0.0)


def _norm(h, w, *, seq, delta=None, gate=None, sc=None, sh=None, router=None, n_experts=0,
          out_dtype=BF16):
    t, d = h.shape
    tm = 256
    per_seq = seq // tm
    row = pl.BlockSpec((tm, d), lambda i: (i, 0))
    vec = pl.BlockSpec((1, d), lambda i: (0, 0))
    bvec = pl.BlockSpec((1, 1, d), lambda i: (i // per_seq, 0, 0))
    args, in_specs = [h], [row]
    if delta is not None:
        args += [delta, gate]
        in_specs += [row, bvec]
    args.append(w.reshape(1, d))
    in_specs.append(vec)
    if sc is not None:
        args += [sc, sh]
        in_specs += [bvec, bvec]
    if router is not None:
        args.append(router)
        in_specs.append(pl.BlockSpec((d, LANE), lambda i: (0, 0)))
    out_shape, out_specs = [], []
    if delta is not None:
        out_shape.append(jax.ShapeDtypeStruct((t, d), F32))
        out_specs.append(row)
    out_shape.append(jax.ShapeDtypeStruct((t, d), out_dtype))
    out_specs.append(row)
    if router is not None:
        out_shape += [jax.ShapeDtypeStruct((t, LANE), F32)] * 2
        out_specs += [pl.BlockSpec((tm, LANE), lambda i: (i, 0))] * 2
    kern = functools.partial(_norm_kernel, has_delta=delta is not None, has_mod=sc is not None,
                             n_experts=n_experts)
    return pl.pallas_call(kern, grid=(t // tm,), in_specs=in_specs, out_specs=out_specs,
                          out_shape=out_shape, compiler_params=_cparams("parallel"), name="norm")(*args)


def _rope_group(x, c, s1, s2, half):
    return x * c + pltpu.roll(x, LANE - half, axis=1) * s1 + pltpu.roll(x, half, axis=1) * s2


def _mm_kernel(*refs, mode, n_q, n_qk, q_scale, rope_half):
    if mode == "rope":
        a_ref, w_ref, c_ref, s1_ref, s2_ref, o_ref = refs
    elif mode == "resid":
        a_ref, w_ref, h_ref, g_ref, o_ref = refs
    else:
        a_ref, w_ref, o_ref = refs
    a = a_ref[...]
    groups = [slice(n * MXU_N, (n + 1) * MXU_N) for n in range(o_ref.shape[1] // MXU_N)]
    if mode == "resid":
        for cs in groups:
            acc = jnp.dot(a, w_ref[:, cs], preferred_element_type=F32)
            o_ref[:, cs] = h_ref[:, cs] + g_ref[0, :, cs] * acc
    elif mode == "rope":
        j = pl.program_id(1)

        @pl.when(j < n_qk)
        def _():
            c, s1, s2 = c_ref[...], s1_ref[...], s2_ref[...]
            scale = jnp.where(j < n_q, q_scale, 1.0)
            for cs in groups:
                acc = jnp.dot(a, w_ref[:, cs], preferred_element_type=F32)
                for g in range(MXU_N // LANE):
                    sl = slice(cs.start + g * LANE, cs.start + (g + 1) * LANE)
                    xg = acc[:, g * LANE:(g + 1) * LANE]
                    o_ref[:, sl] = (_rope_group(xg, c, s1, s2, rope_half) * scale).astype(o_ref.dtype)

        @pl.when(j >= n_qk)
        def _():
            for cs in groups:
                o_ref[:, cs] = jnp.dot(a, w_ref[:, cs], preferred_element_type=F32).astype(o_ref.dtype)
    else:
        for cs in groups:
            o_ref[:, cs] = jnp.dot(a, w_ref[:, cs], preferred_element_type=F32).astype(o_ref.dtype)


def _matmul(a, w, *, out_dtype=BF16, mode="plain", seq=None, h=None, gate=None, rope=None,
            rope_cols=(0, 0), rope_half=0, q_scale=1.0, tm=1024, tn=512):
    m, k = a.shape
    n = w.shape[1]
    tm, tn = min(tm, m, seq or m), min(tn, n)
    in_specs = [pl.BlockSpec((tm, k), lambda i, j: (i, 0)), pl.BlockSpec((k, tn), lambda i, j: (0, j))]
    args = [a, w]
    if mode == "rope":
        tab = pl.BlockSpec((tm, LANE), lambda i, j: (i, 0))
        in_specs += [tab, tab, tab]
        args += list(rope)
    elif mode == "resid":
        per_seq = seq // tm
        in_specs += [pl.BlockSpec((tm, tn), lambda i, j: (i, j)),
                     pl.BlockSpec((1, 1, tn), lambda i, j: (i // per_seq, 0, j))]
        args += [h, gate]
    kern = functools.partial(_mm_kernel, mode=mode, n_q=rope_cols[0] // tn, n_qk=rope_cols[1] // tn,
                             q_scale=q_scale, rope_half=rope_half)
    return pl.pallas_call(
        kern, grid=(m // tm, n // tn), in_specs=in_specs,
        out_specs=pl.BlockSpec((tm, tn), lambda i, j: (i, j)),
        out_shape=jax.ShapeDtypeStruct((m, n), out_dtype),
        compiler_params=_cparams("parallel", "parallel"), name="proj_" + mode,
    )(*args)


def _conv_silu(x, prev, w, kw):
    head = jnp.concatenate([prev, x[:8]], axis=0)
    y = x * w[kw - 1:kw]
    y_head = head * w[kw - 1:kw]
    for s in range(1, kw):
        y = y + pltpu.roll(x, s, axis=0) * w[kw - 1 - s:kw - s]
        y_head = y_head + pltpu.roll(head, s, axis=0) * w[kw - 1 - s:kw - s]
    return _silu(jnp.concatenate([y_head[8:], y[8:]], axis=0))


def _gdn_proj_kernel(a_ref, w_ref, cw_ref, o_ref, carry_ref, *, n_q, n_qk, n_conv, per_seq, kw):
    i, j = pl.program_id(0), pl.program_id(1)
    tm = a_ref.shape[0]
    a = a_ref[...]
    groups = [slice(n * MXU_N, (n + 1) * MXU_N) for n in range(o_ref.shape[1] // MXU_N)]

    @pl.when(i == 0)
    def _():
        carry_ref[j] = jnp.zeros(carry_ref.shape[1:], F32)

    def conv_tile(kind):
        first = i % per_seq == 0
        for cs in groups:
            acc = jnp.dot(a, w_ref[:, cs], preferred_element_type=F32)
            prev = jnp.where(first, 0.0, carry_ref[j, :, cs])
            carry_ref[j, :, cs] = acc[tm - 8:]
            y = _conv_silu(acc, prev, cw_ref[:, cs], kw)
            if kind == "v":
                o_ref[:, cs] = y.astype(o_ref.dtype)
            else:
                scale = GDN_HEAD_DIM ** -0.5 if kind == "q" else 1.0
                for g in range(MXU_N // GDN_HEAD_DIM):
                    yg = y[:, g * GDN_HEAD_DIM:(g + 1) * GDN_HEAD_DIM]
                    yn = yg * lax.rsqrt(jnp.sum(yg * yg, axis=-1, keepdims=True) + L2_EPS)
                    sl = slice(cs.start + g * GDN_HEAD_DIM, cs.start + (g + 1) * GDN_HEAD_DIM)
                    o_ref[:, sl] = (yn * scale).astype(o_ref.dtype)

    pl.when(j < n_q)(functools.partial(conv_tile, "q"))
    pl.when((j >= n_q) & (j < n_qk))(functools.partial(conv_tile, "k"))
    pl.when((j >= n_qk) & (j < n_conv))(functools.partial(conv_tile, "v"))

    @pl.when(j >= n_conv)
    def _():
        for cs in groups:
            o_ref[:, cs] = jnp.dot(a, w_ref[:, cs], preferred_element_type=F32).astype(o_ref.dtype)


def _gdn_proj(a, w, conv_w, *, key_dim, seq, tm=1024, tn=1024):
    m, k = a.shape
    n = w.shape[1]
    kw, conv_dim = conv_w.shape
    tm, tn = min(tm, m, seq), min(tn, n)
    n_conv = conv_dim // tn
    kern = functools.partial(_gdn_proj_kernel, n_q=key_dim // tn, n_qk=2 * key_dim // tn, n_conv=n_conv,
                             per_seq=seq // tm, kw=kw)
    return pl.pallas_call(
        kern, grid=(m // tm, n // tn),
        in_specs=[pl.BlockSpec((tm, k), lambda i, j: (i, 0)),
                  pl.BlockSpec((k, tn), lambda i, j: (0, j)),
                  pl.BlockSpec((kw, tn), lambda i, j: (0, jnp.minimum(j, n_conv - 1)))],
        out_specs=pl.BlockSpec((tm, tn), lambda i, j: (i, j)),
        out_shape=jax.ShapeDtypeStruct((m, n), BF16),
        scratch_shapes=[pltpu.VMEM((n // tn, 8, tn), F32)],
        compiler_params=_cparams("arbitrary", "arbitrary"), name="gdn_proj",
    )(a, w, conv_w)


def _attn_kernel(lam_ref, q_ref, k_ref, v_ref, sub_ref, o_ref, m_ref, l_ref, acc_ref, alpha_ref, p_ref, *,
                 blk, rq, rc, dh, lambda_init):
    qi, ki = pl.program_id(2), pl.program_id(3)

    @pl.when(ki == 0)
    def _():
        m_ref[...] = jnp.full(m_ref.shape, NEG, F32)
        l_ref[...] = jnp.zeros(l_ref.shape, F32)
        acc_ref[...] = jnp.zeros(acc_ref.shape, F32)

    def step(n_keys, diag_key0):
        v = v_ref[:n_keys]
        n_t = n_keys // LANE
        if diag_key0 is not None:
            row = lax.broadcasted_iota(I32, (rc, LANE), 0)
            col = lax.broadcasted_iota(I32, (rc, LANE), 1)
        for q0 in range(0, blk, rq):
            for j in range(2):
                sl = slice(j * dh, (j + 1) * dh)
                s = lax.dot_general(q_ref[q0:q0 + rq, sl], k_ref[:n_keys, sl], (((1,), (1,)), ((), ())),
                                    preferred_element_type=F32)
                for r0 in range(0, rq, rc):
                    rs = slice(q0 + r0, q0 + r0 + rc)
                    tiles = [s[r0:r0 + rc, n * LANE:(n + 1) * LANE] for n in range(n_t)]
                    if diag_key0 is not None:
                        tiles = [tl if n * LANE < diag_key0 else
                                 jnp.where(col + (n * LANE - diag_key0) <= row + (q0 + r0), tl, NEG)
                                 for n, tl in enumerate(tiles)]
                    mx = tiles[0]
                    for tl in tiles[1:]:
                        mx = jnp.maximum(mx, tl)
                    m_prev = m_ref[j, rs]
                    m_new = jnp.maximum(m_prev, jnp.max(mx, axis=1, keepdims=True))
                    alpha = jnp.exp2(m_prev - m_new)
                    ps = [jnp.exp2(tl - m_new) for tl in tiles]
                    psum = ps[0]
                    for pt in ps[1:]:
                        psum = psum + pt
                    l_ref[j, rs] = alpha * l_ref[j, rs] + psum
                    m_ref[j, rs] = m_new
                    alpha_ref[j, rs] = alpha
                    p_ref[j, rs, :n_keys] = jnp.concatenate(ps, axis=1).astype(BF16)
                qs = slice(q0, q0 + rq)
                pv = jnp.dot(p_ref[j, qs, :n_keys], v, preferred_element_type=F32)
                for n in range(pv.shape[1] // LANE):
                    vl = slice(n * LANE, (n + 1) * LANE)
                    acc_ref[j, qs, vl] = alpha_ref[j, qs] * acc_ref[j, qs, vl] + pv[:, vl]

    last = qi // 2

    @pl.when(ki < last)
    def _():
        step(2 * blk, None)

    @pl.when((ki == last) & (qi % 2 == 1))
    def _():
        step(2 * blk, blk)

    @pl.when((ki == last) & (qi % 2 == 0))
    def _():
        step(blk, 0)

    @pl.when(ki == pl.num_programs(3) - 1)
    def _():
        lv = lam_ref[...]
        lam = (jnp.exp(jnp.sum(lv[0:1] * lv[1:2], axis=1, keepdims=True))
               - jnp.exp(jnp.sum(lv[2:3] * lv[3:4], axis=1, keepdims=True)) + lambda_init)
        l0 = jnp.sum(l_ref[0], axis=1, keepdims=True)
        l1 = jnp.sum(l_ref[1], axis=1, keepdims=True)
        o = acc_ref[0] / l0 - lam * (acc_ref[1] / l1)
        o = o * lax.rsqrt(jnp.mean(o * o, axis=-1, keepdims=True) + RMS_EPS) * sub_ref[...]
        o_ref[...] = (o * (1.0 - lambda_init)).astype(o_ref.dtype)


def _diff_attention(qkv, lam_vec, subln, *, batch, seq, layer_idx):
    t, d3 = qkv.shape
    d = d3 // 3
    dv = d // DA_HEADS
    dh = dv // 2
    blk = 512
    nb = seq // blk
    lambda_init = 0.8 - 0.6 * math.exp(-0.3 * layer_idx)
    kern = functools.partial(_attn_kernel, blk=blk, rq=128, rc=128, dh=dh, lambda_init=lambda_init)
    nkb = nb // 2
    kv_row = lambda b, qi, ki: b * nkb + jnp.minimum(ki, qi // 2)
    return pl.pallas_call(
        kern, grid=(batch, DA_HEADS, nb, nkb),
        in_specs=[pl.BlockSpec((4, dh), lambda b, h, qi, ki: (0, 0)),
                  pl.BlockSpec((blk, dv), lambda b, h, qi, ki: (b * nb + qi, h)),
                  pl.BlockSpec((2 * blk, dv), lambda b, h, qi, ki: (kv_row(b, qi, ki), DA_HEADS + h)),
                  pl.BlockSpec((2 * blk, dv), lambda b, h, qi, ki: (kv_row(b, qi, ki), 2 * DA_HEADS + h)),
                  pl.BlockSpec((1, dv), lambda b, h, qi, ki: (0, 0))],
        out_specs=pl.BlockSpec((blk, dv), lambda b, h, qi, ki: (b * nb + qi, h)),
        out_shape=jax.ShapeDtypeStruct((t, d), BF16),
        scratch_shapes=[pltpu.VMEM((2, blk, LANE), F32), pltpu.VMEM((2, blk, LANE), F32),
                        pltpu.VMEM((2, blk, dv), F32), pltpu.VMEM((2, blk, LANE), F32),
                        pltpu.VMEM((2, blk, 2 * blk), BF16)],
        compiler_params=_cparams("parallel", "parallel", "parallel", "arbitrary"), name="diff_attn",
    )(lam_vec, qkv, qkv, qkv, subln.reshape(1, dv))


def _split3(x):
    hi = x.astype(BF16)
    r = x - hi.astype(F32)
    mid = r.astype(BF16)
    lo = (r - mid.astype(F32)).astype(BF16)
    return hi, mid, lo


def _gates_kernel(ba_ref, alog_ref, dt_ref, beta_ref, gc_ref):
    x = ba_ref[...]
    tm = x.shape[0]
    b_raw, a_raw = x[:, :LANE], x[:, LANE:]
    beta_ref[...] = 1.0 / (1.0 + jnp.exp(-b_raw))
    z = a_raw + dt_ref[...]
    softplus = jnp.maximum(z, 0.0) + jnp.log(1.0 + jnp.exp(-jnp.abs(z)))
    g = -jnp.exp(alog_ref[...]) * softplus
    row = lax.broadcasted_iota(I32, (tm, tm), 0)
    col = lax.broadcasted_iota(I32, (tm, tm), 1)
    tri = jnp.where((row >= col) & (row // GDN_CHUNK == col // GDN_CHUNK), 1.0, 0.0).astype(BF16)
    gc_ref[...] = sum(jnp.dot(tri, part, preferred_element_type=F32) for part in _split3(g))


def _gdn_gates(ba, a_log, dt_bias):
    t = ba.shape[0]
    hv = a_log.shape[0]
    tm = 256
    pad = lambda v: jnp.pad(v.astype(F32), (0, LANE - hv)).reshape(1, LANE)
    blk = pl.BlockSpec((tm, LANE), lambda i: (i, 0))
    vec = pl.BlockSpec((1, LANE), lambda i: (0, 0))
    return pl.pallas_call(
        _gates_kernel, grid=(t // tm,),
        in_specs=[pl.BlockSpec((tm, 2 * LANE), lambda i: (i, 0)), vec, vec],
        out_specs=[blk, blk],
        out_shape=[jax.ShapeDtypeStruct((t, LANE), F32)] * 2,
        compiler_params=_cparams("parallel"), name="gdn_gates",
    )(ba, pad(a_log), pad(dt_bias))


def _delta_kernel(q_ref, k_ref, v_ref, z_ref, beta_ref, gc_ref, gct_ref, nw_ref, o_ref, state_ref, *,
                  heads, rows):
    hg, blk = pl.program_id(1), pl.program_id(2)
    dk = GDN_HEAD_DIM
    c = GDN_CHUNK
    n_chunks = rows // c

    @pl.when(blk == 0)
    def _():
        state_ref[...] = jnp.zeros(state_ref.shape, F32)

    n_pairs = rows // LANE
    ri = lax.broadcasted_iota(I32, (c, LANE), 0)
    li = lax.broadcasted_iota(I32, (c, LANE), 1)
    lj = li & (c - 1)
    left = li < c
    strict = ri > lj
    causal = ri >= lj
    same16 = (ri >> 4) == (lj >> 4)
    same32 = (ri >> 5) == (lj >> 5)
    m16 = strict & same16
    m32 = strict & same32 & jnp.logical_not(same16)
    m64 = strict & jnp.logical_not(same32)
    eye = jnp.where(ri == lj, 1.0, 0.0)
    bd_mask = ((lax.broadcasted_iota(I32, (LANE, LANE), 0) < c)
               == (lax.broadcasted_iota(I32, (LANE, LANE), 1) < c))
    lane = lax.broadcasted_iota(I32, (rows, LANE), 1)
    contract_last = (((1,), (1,)), ((), ()))
    contract_first = (((0,), (0,)), ((), ()))

    def bd(p):
        return jnp.where(bd_mask, jnp.concatenate([p, p], axis=0), 0.0).astype(BF16)

    def pm(p, q):
        return jnp.dot(p.astype(BF16), bd(q), preferred_element_type=F32)

    def pack(full):
        return jnp.where(left, full[:c], full[c:])

    gc_all, beta_all = gc_ref[...], beta_ref[...]
    eg_all = jnp.exp(gc_all)
    pairs = [(g, p) for g in range(heads) for p in range(n_pairs)]
    prow = lambda x, p: x[p * LANE:(p + 1) * LANE]

    kkt, qkt = {}, {}
    for kh in range(heads // 2):
        ks = slice(kh * dk, (kh + 1) * dk)
        for p in range(n_pairs):
            kp = k_ref[p * LANE:(p + 1) * LANE, ks]
            qp = q_ref[p * LANE:(p + 1) * LANE, ks]
            kkt[kh, p] = pack(lax.dot_general(kp, kp, contract_last, preferred_element_type=F32))
            qkt[kh, p] = pack(lax.dot_general(qp, kp, contract_last, preferred_element_type=F32))

    hd = []
    for g in range(heads):
        pick = lane == hg * heads + g
        col = lambda x: jnp.sum(jnp.where(pick, x, 0.0), axis=1, keepdims=True)
        gc_col, beta_col, eg_col = col(gc_all), col(beta_all), col(eg_all)
        ks = slice((g // 2) * dk, (g // 2 + 1) * dk)
        kf = k_ref[:, ks].astype(F32)
        k_beta = kf * beta_col
        y = jnp.concatenate([v_ref[:, g * dk:(g + 1) * dk].astype(F32) * beta_col, k_beta * eg_col],
                            axis=1).astype(BF16)
        gl = [gc_col[ci * c + c - 1:ci * c + c] for ci in range(n_chunks)]
        gl_col = jnp.concatenate([jnp.broadcast_to(v, (c, 1)) for v in gl], axis=0)
        hd.append(dict(gc=gc_col, beta=beta_col, y=y,
                       q_dec=(q_ref[:, ks].astype(F32) * eg_col).astype(BF16),
                       k_dec=(kf * jnp.exp(gl_col - gc_col)).astype(BF16),
                       g_end=[jnp.exp(v) for v in gl]))

    a, intra = {}, {}
    for g, p in pairs:
        gcp = jnp.where(left, prow(hd[g]["gc"], p)[:c], prow(hd[g]["gc"], p)[c:])
        bp = jnp.where(left, prow(hd[g]["beta"], p)[:c], prow(hd[g]["beta"], p)[c:])
        decay = jnp.exp(jnp.where(causal, gcp - gct_ref[g][:, p * LANE:(p + 1) * LANE], NEG))
        a[g, p] = kkt[g // 2, p] * bp * decay
        intra[g, p] = qkt[g // 2, p] * decay

    d1 = {n: jnp.where(m16, a[n], 0.0) for n in pairs}
    d2 = {n: pm(d1[n], d1[n]) for n in pairs}
    d4 = {n: pm(d2[n], d2[n]) for n in pairs}
    x = {n: eye - d1[n] for n in pairs}
    x = {n: x[n] + pm(x[n], d2[n]) for n in pairs}
    d8 = {n: pm(d4[n], d4[n]) for n in pairs}
    x = {n: x[n] + pm(x[n], d4[n]) for n in pairs}
    x = {n: x[n] + pm(x[n], d8[n]) for n in pairs}
    for mask in (m32, m64):
        t = {n: pm(jnp.where(mask, a[n], 0.0), x[n]) for n in pairs}
        x = {n: x[n] - pm(x[n], t[n]) for n in pairs}
    uw = {(g, p): jnp.dot(bd(x[g, p]), prow(hd[g]["y"], p), preferred_element_type=F32) for g, p in pairs}

    s = [state_ref[g] for g in range(heads)]
    v_new = {}
    o_state = {}
    for ci in range(n_chunks):
        p, half = divmod(ci, 2)
        hs = slice(half * c, (half + 1) * c)
        rs = slice(ci * c, (ci + 1) * c)
        sb = [s[g].astype(BF16) for g in range(heads)]
        for g in range(heads):
            u_c, w_c = uw[g, p][hs, :dk], uw[g, p][hs, dk:]
            v_new[g, ci] = u_c - jnp.dot(w_c.astype(BF16), sb[g], preferred_element_type=F32)
            o_state[g, ci] = jnp.dot(hd[g]["q_dec"][rs], sb[g], preferred_element_type=F32)
        for g in range(heads):
            s[g] = s[g] * hd[g]["g_end"][ci] + lax.dot_general(
                hd[g]["k_dec"][rs], v_new[g, ci].astype(BF16), contract_first, preferred_element_type=F32)
    for g in range(heads):
        state_ref[g] = s[g]

    for g in range(heads):
        o_pairs = []
        for p in range(n_pairs):
            vn = jnp.concatenate([v_new[g, 2 * p], v_new[g, 2 * p + 1]], axis=0).astype(BF16)
            o_in = jnp.dot(bd(intra[g, p]), vn, preferred_element_type=F32)
            o_pairs.append(o_in + jnp.concatenate([o_state[g, 2 * p], o_state[g, 2 * p + 1]], axis=0))
        o = jnp.concatenate(o_pairs, axis=0)
        o = o * lax.rsqrt(jnp.mean(o * o, axis=-1, keepdims=True) + RMS_EPS) * nw_ref[...]
        vs = slice(g * dk, (g + 1) * dk)
        o_ref[:, vs] = (o * _silu(z_ref[:, vs].astype(F32))).astype(o_ref.dtype)


def _delta_rule(proj, key_dim, val_dim, beta, gc, gct, norm_w, *, batch, seq):
    t = proj.shape[0]
    vdim = val_dim
    dk = GDN_HEAD_DIM
    hv = vdim // dk
    heads = 8
    rows = 256
    nblk = seq // rows
    qw, vw = heads // 2 * dk, heads * dk
    kb, vb, zb = key_dim // qw, 2 * key_dim // vw, (2 * key_dim + val_dim) // vw
    kern = functools.partial(_delta_kernel, heads=heads, rows=rows)
    tok = lambda b, hg, i: (b * nblk + i, 0)
    return pl.pallas_call(
        kern, grid=(batch, hv // heads, nblk),
        in_specs=[pl.BlockSpec((rows, qw), lambda b, hg, i: (b * nblk + i, hg)),
                  pl.BlockSpec((rows, qw), lambda b, hg, i: (b * nblk + i, kb + hg)),
                  pl.BlockSpec((rows, vw), lambda b, hg, i: (b * nblk + i, vb + hg)),
                  pl.BlockSpec((rows, vw), lambda b, hg, i: (b * nblk + i, zb + hg)),
                  pl.BlockSpec((rows, LANE), tok),
                  pl.BlockSpec((rows, LANE), tok),
                  pl.BlockSpec((heads, 1, rows), lambda b, hg, i: (hg, 0, b * nblk + i)),
                  pl.BlockSpec((1, dk), lambda b, hg, i: (0, 0))],
        out_specs=pl.BlockSpec((rows, vw), lambda b, hg, i: (b * nblk + i, hg)),
        out_shape=jax.ShapeDtypeStruct((t, vdim), BF16),
        scratch_shapes=[pltpu.VMEM((heads, dk, dk), F32)],
        compiler_params=_cparams("parallel", "parallel", "arbitrary"), name="delta_rule",
    )(proj, proj, proj, proj, beta, gc, gct, norm_w.reshape(1, dk))


def _ffn_kernel(te_ref, *refs, mode):
    del te_ref
    if mode == "resid":
        x_ref, wg_ref, wu_ref, wd_ref, h_ref, g_ref, o_ref = refs
        acc_ref = o_ref
    else:
        x_ref, wg_ref, wu_ref, wd_ref, o_ref, acc_ref = refs
    f = pl.program_id(1)

    @pl.when(f == 0)
    def _():
        acc_ref[...] = h_ref[...] if mode == "resid" else jnp.zeros(acc_ref.shape, F32)

    x = x_ref[...]
    tf = wg_ref.shape[2]
    mid = []
    for c0 in range(0, tf, MXU_N):
        cs = slice(c0, min(c0 + MXU_N, tf))
        gate = jnp.dot(x, wg_ref[0, :, cs], preferred_element_type=F32)
        up = jnp.dot(x, wu_ref[0, :, cs], preferred_element_type=F32)
        mid.append((_silu(gate) * up).astype(BF16))
    down = jnp.dot(jnp.concatenate(mid, axis=1), wd_ref[0], preferred_element_type=F32)
    if mode == "resid":
        acc_ref[...] += g_ref[0] * down
    else:
        acc_ref[...] += down

        @pl.when(f == pl.num_programs(1) - 1)
        def _():
            o_ref[...] = acc_ref[...].astype(o_ref.dtype)


def _ffn(x, wg, wu, wd, tile_expert, *, tm, mode, seq=None, h=None, gate=None):
    m, d = x.shape
    ff = wg.shape[2]
    tf = 11 * LANE if ff % (11 * LANE) == 0 else 512
    xs = pl.BlockSpec((tm, d), lambda i, f, te: (i, 0))
    in_specs = [xs,
                pl.BlockSpec((1, d, tf), lambda i, f, te: (te[i], 0, f)),
                pl.BlockSpec((1, d, tf), lambda i, f, te: (te[i], 0, f)),
                pl.BlockSpec((1, tf, d), lambda i, f, te: (te[i], f, 0))]
    args = [x, wg, wu, wd]
    if mode == "resid":
        per_seq = seq // tm
        in_specs += [pl.BlockSpec((tm, d), lambda i, f, te: (i, 0), pipeline_mode=pl.Buffered(1)),
                     pl.BlockSpec((1, 1, d), lambda i, f, te: (i // per_seq, 0, 0))]
        args += [h, gate]
        out_dtype = F32
    else:
        out_dtype = BF16
    return pl.pallas_call(
        functools.partial(_ffn_kernel, mode=mode),
        grid_spec=pltpu.PrefetchScalarGridSpec(
            num_scalar_prefetch=1, grid=(m // tm, ff // tf), in_specs=in_specs, out_specs=xs,
            scratch_shapes=[] if mode == "resid" else [pltpu.VMEM((tm, d), F32)]),
        out_shape=jax.ShapeDtypeStruct((m, d), out_dtype),
        compiler_params=pltpu.CompilerParams(dimension_semantics=("parallel", "arbitrary"),
                                             vmem_limit_bytes=V7X_VMEM_BYTES * 7 // 8),
        name="swiglu_" + mode,
    )(tile_expert, *args)


def _scatter_kernel(sa_ref, sb_ref, x_ref, init_ref, o_ref, sem, *, tm):
    del init_ref
    def issue(r, carry):
        pltpu.make_async_copy(x_ref.at[r], o_ref.at[sa_ref[r]], sem).start()
        pltpu.make_async_copy(x_ref.at[r], o_ref.at[sb_ref[r]], sem).start()
        return carry
    lax.fori_loop(0, tm, issue, 0)
    drain = pltpu.make_async_copy(x_ref, o_ref.at[pl.ds(0, tm)], sem)
    drain.wait()
    drain.wait()


def _scatter_rows(x, slot_a, slot_b, n_slots, *, tm):
    t, s, l = x.shape
    ispec = pl.BlockSpec((tm,), lambda i: (i,), memory_space=pltpu.SMEM)
    return pl.pallas_call(
        functools.partial(_scatter_kernel, tm=tm), grid=(t // tm,),
        in_specs=[ispec, ispec, pl.BlockSpec((tm, s, l), lambda i: (i, 0, 0)), pl.BlockSpec(memory_space=pl.ANY)],
        out_specs=pl.BlockSpec(memory_space=pl.ANY),
        out_shape=jax.ShapeDtypeStruct((n_slots, s, l), x.dtype),
        scratch_shapes=[pltpu.SemaphoreType.DMA(())],
        input_output_aliases={3: 0},
        compiler_params=_cparams("arbitrary"), name="scatter_rows",
    )(slot_a, slot_b, x, jnp.zeros((n_slots, s, l), x.dtype))


def _combine_kernel(ia_ref, ib_ref, wa_ref, wb_ref, src_ref, o_ref, a_buf, b_buf, sem_a, sem_b, *, tm):
    def issue(r, carry):
        pltpu.make_async_copy(src_ref.at[ia_ref[r]], a_buf.at[r], sem_a).start()
        pltpu.make_async_copy(src_ref.at[ib_ref[r]], b_buf.at[r], sem_b).start()
        return carry
    lax.fori_loop(0, tm, issue, 0)
    pltpu.make_async_copy(src_ref.at[pl.ds(0, tm)], a_buf, sem_a).wait()
    pltpu.make_async_copy(src_ref.at[pl.ds(0, tm)], b_buf, sem_b).wait()
    o_ref[...] = wa_ref[...] * a_buf[...].astype(F32) + wb_ref[...] * b_buf[...].astype(F32)


def _combine_rows(src, idx_a, idx_b, w_a, w_b, *, tm):
    _, s, l = src.shape
    t = idx_a.shape[0]
    ispec = pl.BlockSpec((tm,), lambda i: (i,), memory_space=pltpu.SMEM)
    wspec = pl.BlockSpec((tm, 1, l), lambda i: (i, 0, 0))
    return pl.pallas_call(
        functools.partial(_combine_kernel, tm=tm), grid=(t // tm,),
        in_specs=[ispec, ispec, wspec, wspec, pl.BlockSpec(memory_space=pl.ANY)],
        out_specs=pl.BlockSpec((tm, s, l), lambda i: (i, 0, 0)),
        out_shape=jax.ShapeDtypeStruct((t, s, l), F32),
        scratch_shapes=[pltpu.VMEM((tm, s, l), src.dtype), pltpu.VMEM((tm, s, l), src.dtype),
                        pltpu.SemaphoreType.DMA(()), pltpu.SemaphoreType.DMA(())],
        compiler_params=_cparams("arbitrary"), name="combine_rows",
    )(idx_a, idx_b, w_a, w_b, src)


def _dispatch_plan(sel, gates, n_experts, tm):
    t = sel.shape[0]
    chosen = sel[:, :n_experts] > 0.0
    seli = chosen.astype(I32)
    counts = jnp.sum(seli, axis=0)
    padded = (counts + tm - 1) // tm * tm
    ends = jnp.cumsum(padded)
    starts = ends - padded
    rank = jnp.cumsum(seli, axis=0) - seli
    slot = starts[None, :] + rank
    n_slots = t * TOP_K + n_experts * tm
    slot_a = jnp.min(jnp.where(chosen, slot, n_slots), axis=1).astype(I32)
    slot_b = jnp.max(jnp.where(chosen, slot, -1), axis=1).astype(I32)
    g = gates[:, :n_experts]
    lanes = lambda w: jnp.broadcast_to(w[:, None, None], (t, 1, LANE))
    w_a = lanes(jnp.sum(jnp.where(chosen & (slot == slot_a[:, None]), g, 0.0), axis=1))
    w_b = lanes(jnp.sum(jnp.where(chosen & (slot == slot_b[:, None]), g, 0.0), axis=1))
    tile_start = jnp.arange(n_slots // tm, dtype=I32) * tm
    tile_expert = jnp.minimum(jnp.searchsorted(ends, tile_start, side="right"), n_experts - 1).astype(I32)
    return slot_a, slot_b, w_a, w_b, tile_expert, n_slots


def _rope_tables(positions, rot_dim):
    half = rot_dim // 2
    inv_freq = 1.0 / (ROPE_THETA ** (jnp.arange(0, rot_dim, 2, dtype=F32) / rot_dim))
    ang = positions.astype(F32).reshape(-1, 1) * inv_freq
    cos, sin = jnp.cos(ang), jnp.sin(ang)
    t = ang.shape[0]
    ones = jnp.ones((t, LANE - rot_dim), F32)
    zeros = jnp.zeros((t, LANE - half), F32)
    c = jnp.concatenate([cos, cos, ones], axis=1)
    s1 = jnp.concatenate([-sin, zeros], axis=1)
    s2 = jnp.concatenate([zeros[:, :half], sin, zeros[:, :LANE - rot_dim]], axis=1)
    return c, s1, s2


def kernel(x, c, positions, ada_w, ada_b, norm1_w, norm2_w, attn_w_in, attn_lambda, attn_subln, attn_w_out, gdn_w_in, gdn_conv_w, gdn_a_log, gdn_dt_bias, gdn_norm_w, gdn_w_out, ffn_w_gate, ffn_w_up, ffn_w_down, moe_router, moe_w_gate, moe_w_up, moe_w_down, final_norm_w):
    batch, seq, d = x.shape
    t = batch * seq
    depth = ada_w.shape[0]
    n_experts = moe_router.shape[-1]
    dh = d // DA_HEADS // 2
    key_dim = d
    val_dim = gdn_w_out.shape[1]
    conv_dim = gdn_conv_w.shape[-1]
    hv = gdn_a_log.shape[-1]
    moe_tm = 512
    s_sub = d // LANE

    mod = _adaln(c, ada_w, ada_b)
    rope = _rope_tables(positions, dh // 4)
    h = x.reshape(t, d)
    delta = gate = None
    ffn_w = [w.astype(BF16) for w in (ffn_w_gate, ffn_w_up, ffn_w_down)]
    moe_w = [w.astype(BF16).reshape((-1,) + w.shape[2:]) for w in (moe_w_gate, moe_w_up, moe_w_down)]

    for i in range(depth):
        j = i // 2
        sh1, sc1, g1, sh2, sc2, g2 = [mod[i, :, n * d:(n + 1) * d].reshape(batch, 1, d) for n in range(6)]
        if delta is None:
            u = _norm(h, norm1_w[i], seq=seq, sc=sc1, sh=sh1)[0]
        else:
            h, u = _norm(h, norm1_w[i], seq=seq, delta=delta, gate=gate, sc=sc1, sh=sh1)
            delta = gate = None
        if i % 2 == 0:
            qkv = _matmul(u, attn_w_in[j].astype(BF16), mode="rope", rope=rope, rope_cols=(d, 2 * d),
                          rope_half=dh // 8, q_scale=dh ** -0.5 * math.log2(math.e), tn=1024)
            o = _diff_attention(qkv, attn_lambda[j], attn_subln[j], batch=batch, seq=seq, layer_idx=i)
            h = _matmul(o, attn_w_out[j].astype(BF16), mode="resid", seq=seq, h=h, gate=g1, out_dtype=F32)
        else:
            w_in = gdn_w_in[j]
            n_main = conv_dim + val_dim
            proj = _gdn_proj(u, w_in[:, :n_main].astype(BF16), gdn_conv_w[j], key_dim=key_dim, seq=seq)
            w_ba = jnp.zeros((d, 2 * LANE), F32)
            w_ba = w_ba.at[:, :hv].set(w_in[:, n_main:n_main + hv]).at[:, LANE:LANE + hv].set(w_in[:, n_main + hv:])
            ba = _matmul(u, w_ba.astype(BF16), out_dtype=F32, tn=2 * LANE)
            beta, gc = _gdn_gates(ba, gdn_a_log[j], gdn_dt_bias[j])
            gct = gc[:, :hv].T.reshape(hv, 1, t)
            o = _delta_rule(proj, key_dim, val_dim, beta, gc, gct, gdn_norm_w[j], batch=batch, seq=seq)
            h = _matmul(o, gdn_w_out[j].astype(BF16), mode="resid", seq=seq, h=h, gate=g1, out_dtype=F32)
        if i % 2 == 0:
            u = _norm(h, norm2_w[i], seq=seq, sc=sc2, sh=sh2)[0]
            h = _ffn(u, *ffn_w, jnp.full((t // moe_tm,), j, I32), tm=moe_tm, mode="resid", seq=seq, h=h, gate=g2)
        else:
            router = jnp.pad(moe_router[j], ((0, 0), (0, LANE - n_experts)))
            u, gates, sel = _norm(h, norm2_w[i], seq=seq, sc=sc2, sh=sh2, router=router, n_experts=n_experts)
            slot_a, slot_b, w_a, w_b, tile_expert, n_slots = _dispatch_plan(sel, gates, n_experts, moe_tm)
            xs = _scatter_rows(u.reshape(t, s_sub, LANE), slot_a, slot_b, n_slots, tm=moe_tm)
            ys = _ffn(xs.reshape(-1, d), *moe_w, tile_expert + j * n_experts, tm=moe_tm, mode="plain")
            delta = _combine_rows(ys.reshape(-1, s_sub, LANE), slot_a, slot_b, w_a, w_b, tm=moe_tm).reshape(t, d)
            gate = g2
    if delta is None:
        out = _norm(h, final_norm_w, seq=seq, out_dtype=F32)[0]
    else:
        out = _norm(h, final_norm_w, seq=seq, delta=delta, gate=gate, out_dtype=F32)[1]
    return out.reshape(batch, seq, d)
```

```python
import functools
import math

import jax
import jax.numpy as jnp
from jax import lax
from jax.experimental import pallas as pl
from jax.experimental.pallas import tpu as pltpu

F32 = jnp.float32
BF16 = jnp.bfloat16
I32 = jnp.int32

LANE = 128
MXU_N = 256
V7X_VMEM_BYTES = 64 * 1024 * 1024
VMEM_LIMIT = V7X_VMEM_BYTES * 3 // 4

RMS_EPS = 1e-6
DA_HEADS = 8
ROPE_THETA = 500000.0
GDN_HEAD_DIM = 128
GDN_CHUNK = 64
L2_EPS = 1e-6
TOP_K = 2
NEG = -1e30


def _cparams(*sem):
    return pltpu.CompilerParams(dimension_semantics=sem, vmem_limit_bytes=VMEM_LIMIT)


def _silu(x):
    return x / (1.0 + jnp.exp(-x))


def _adaln_kernel(c_ref, w_ref, b_ref, o_ref):
    ca = _silu(c_ref[...]).astype(BF16)
    o_ref[0] = jnp.dot(ca, w_ref[0].astype(BF16), preferred_element_type=F32) + b_ref[0]


def _adaln(c, ada_w, ada_b):
    depth, d, n = ada_w.shape
    b = c.shape[0]
    tn = 1024
    return pl.pallas_call(
        _adaln_kernel,
        grid=(depth, n // tn),
        in_specs=[pl.BlockSpec((b, d), lambda l, j: (0, 0)),
                  pl.BlockSpec((1, d, tn), lambda l, j: (l, 0, j)),
                  pl.BlockSpec((1, 1, tn), lambda l, j: (l, 0, j))],
        out_specs=pl.BlockSpec((1, b, tn), lambda l, j: (l, 0, j)),
        out_shape=jax.ShapeDtypeStruct((depth, b, n), F32),
        compiler_params=_cparams("parallel", "parallel"), name="adaln",
    )(c, ada_w, ada_b.reshape(depth, 1, n))


def _norm_kernel(*refs, has_delta, has_mod, n_experts):
    refs = list(refs)
    h_ref = refs.pop(0)
    if has_delta:
        d_ref, g_ref = refs.pop(0), refs.pop(0)
    w_ref = refs.pop(0)
    if has_mod:
        sc_ref, sh_ref = refs.pop(0), refs.pop(0)
    if n_experts:
        r_ref = refs.pop(0)
    if has_delta:
        hn_ref = refs.pop(0)
    u_ref = refs.pop(0)
    if n_experts:
        gates_ref, sel_ref = refs.pop(0), refs.pop(0)

    x = h_ref[...]
    if has_delta:
        x = x + g_ref[0] * d_ref[...]
        hn_ref[...] = x
    y = x * lax.rsqrt(jnp.mean(x * x, axis=-1, keepdims=True) + RMS_EPS) * w_ref[...]
    if has_mod:
        y = y * (1.0 + sc_ref[0]) + sh_ref[0]
    u_ref[...] = y.astype(u_ref.dtype)
    if n_experts:
        logits = jnp.dot(y, r_ref[...], preferred_element_type=F32, precision=lax.Precision.HIGHEST)
        lane = lax.broadcasted_iota(I32, logits.shape, 1)
        logits = jnp.where(lane < n_experts, logits, -jnp.inf)
        m1 = jnp.max(logits, axis=1, keepdims=True)
        i1 = jnp.min(jnp.where(logits == m1, lane, LANE), axis=1, keepdims=True)
        oh1 = lane == i1
        rest = jnp.where(oh1, -jnp.inf, logits)
        m2 = jnp.max(rest, axis=1, keepdims=True)
        i2 = jnp.min(jnp.where(rest == m2, lane, LANE), axis=1, keepdims=True)
        oh2 = lane == i2
        e = jnp.exp(m2 - m1)
        gates_ref[...] = jnp.where(oh1, 1.0 / (1.0 + e), 0.0) + jnp.where(oh2, e / (1.0 + e), 0.0)
        sel_ref[...] = jnp.where(oh1 | oh2, 1.0, 0.0)


def _norm(h, w, *, seq, delta=None, gate=None, sc=None, sh=None, router=None, n_experts=0,
          out_dtype=BF16):
    t, d = h.shape
    tm = 256
    per_seq = seq // tm
    row = pl.BlockSpec((tm, d), lambda i: (i, 0))
    vec = pl.BlockSpec((1, d), lambda i: (0, 0))
    bvec = pl.BlockSpec((1, 1, d), lambda i: (i // per_seq, 0, 0))
    args, in_specs = [h], [row]
    if delta is not None:
        args += [delta, gate]
        in_specs += [row, bvec]
    args.append(w.reshape(1, d))
    in_specs.append(vec)
    if sc is not None:
        args += [sc, sh]
        in_specs += [bvec, bvec]
    if router is not None:
        args.append(router)
        in_specs.append(pl.BlockSpec((d, LANE), lambda i: (0, 0)))
    out_shape, out_specs = [], []
    if delta is not None:
        out_shape.append(jax.ShapeDtypeStruct((t, d), F32))
        out_specs.append(row)
    out_shape.append(jax.ShapeDtypeStruct((t, d), out_dtype))
    out_specs.append(row)
    if router is not None:
        out_shape += [jax.ShapeDtypeStruct((t, LANE), F32)] * 2
        out_specs += [pl.BlockSpec((tm, LANE), lambda i: (i, 0))] * 2
    kern = functools.partial(_norm_kernel, has_delta=delta is not None, has_mod=sc is not None,
                             n_experts=n_experts)
    return pl.pallas_call(kern, grid=(t // tm,), in_specs=in_specs, out_specs=out_specs,
                          out_shape=out_shape, compiler_params=_cparams("parallel"), name="norm")(*args)


def _rope_group(x, c, s1, s2, half):
    return x * c + pltpu.roll(x, LANE - half, axis=1) * s1 + pltpu.roll(x, half, axis=1) * s2


def _mm_kernel(*refs, mode, n_q, n_qk, q_scale, rope_half):
    if mode == "rope":
        a_ref, w_ref, c_ref, s1_ref, s2_ref, o_ref = refs
    elif mode == "resid":
        a_ref, w_ref, h_ref, g_ref, o_ref = refs
    else:
        a_ref, w_ref, o_ref = refs
    a = a_ref[...]
    groups = [slice(n * MXU_N, (n + 1) * MXU_N) for n in range(o_ref.shape[1] // MXU_N)]
    if mode == "resid":
        for cs in groups:
            acc = jnp.dot(a, w_ref[:, cs], preferred_element_type=F32)
            o_ref[:, cs] = h_ref[:, cs] + g_ref[0, :, cs] * acc
    elif mode == "rope":
        j = pl.program_id(1)

        @pl.when(j < n_qk)
        def _():
            c, s1, s2 = c_ref[...], s1_ref[...], s2_ref[...]
            scale = jnp.where(j < n_q, q_scale, 1.0)
            for cs in groups:
                acc = jnp.dot(a, w_ref[:, cs], preferred_element_type=F32)
                for g in range(MXU_N // LANE):
                    sl = slice(cs.start + g * LANE, cs.start + (g + 1) * LANE)
                    xg = acc[:, g * LANE:(g + 1) * LANE]
                    o_ref[:, sl] = (_rope_group(xg, c, s1, s2, rope_half) * scale).astype(o_ref.dtype)

        @pl.when(j >= n_qk)
        def _():
            for cs in groups:
                o_ref[:, cs] = jnp.dot(a, w_ref[:, cs], preferred_element_type=F32).astype(o_ref.dtype)
    else:
        for cs in groups:
            o_ref[:, cs] = jnp.dot(a, w_ref[:, cs], preferred_element_type=F32).astype(o_ref.dtype)


def _matmul(a, w, *, out_dtype=BF16, mode="plain", seq=None, h=None, gate=None, rope=None,
            rope_cols=(0, 0), rope_half=0, q_scale=1.0, tm=1024, tn=512):
    m, k = a.shape
    n = w.shape[1]
    tm, tn = min(tm, m, seq or m), min(tn, n)
    in_specs = [pl.BlockSpec((tm, k), lambda i, j: (i, 0)), pl.BlockSpec((k, tn), lambda i, j: (0, j))]
    args = [a, w]
    if mode == "rope":
        tab = pl.BlockSpec((tm, LANE), lambda i, j: (i, 0))
        in_specs += [tab, tab, tab]
        args += list(rope)
    elif mode == "resid":
        per_seq = seq // tm
        in_specs += [pl.BlockSpec((tm, tn), lambda i, j: (i, j)),
                     pl.BlockSpec((1, 1, tn), lambda i, j: (i // per_seq, 0, j))]
        args += [h, gate]
    kern = functools.partial(_mm_kernel, mode=mode, n_q=rope_cols[0] // tn, n_qk=rope_cols[1] // tn,
                             q_scale=q_scale, rope_half=rope_half)
    return pl.pallas_call(
        kern, grid=(m // tm, n // tn), in_specs=in_specs,
        out_specs=pl.BlockSpec((tm, tn), lambda i, j: (i, j)),
        out_shape=jax.ShapeDtypeStruct((m, n), out_dtype),
        compiler_params=_cparams("parallel", "parallel"), name="proj_" + mode,
    )(*args)


def _conv_silu(x, prev, w, kw):
    head = jnp.concatenate([prev, x[:8]], axis=0)
    y = x * w[kw - 1:kw]
    y_head = head * w[kw - 1:kw]
    for s in range(1, kw):
        y = y + pltpu.roll(x, s, axis=0) * w[kw - 1 - s:kw - s]
        y_head = y_head + pltpu.roll(head, s, axis=0) * w[kw - 1 - s:kw - s]
    return _silu(jnp.concatenate([y_head[8:], y[8:]], axis=0))


def _gdn_proj_kernel(a_ref, w_ref, cw_ref, o_ref, carry_ref, *, n_q, n_qk, n_conv, per_seq, kw):
    i, j = pl.program_id(0), pl.program_id(1)
    tm = a_ref.shape[0]
    a = a_ref[...]
    groups = [slice(n * MXU_N, (n + 1) * MXU_N) for n in range(o_ref.shape[1] // MXU_N)]

    @pl.when(i == 0)
    def _():
        carry_ref[j] = jnp.zeros(carry_ref.shape[1:], F32)

    def conv_tile(kind):
        first = i % per_seq == 0
        for cs in groups:
            acc = jnp.dot(a, w_ref[:, cs], preferred_element_type=F32)
            prev = jnp.where(first, 0.0, carry_ref[j, :, cs])
            carry_ref[j, :, cs] = acc[tm - 8:]
            y = _conv_silu(acc, prev, cw_ref[:, cs], kw)
            if kind == "v":
                o_ref[:, cs] = y.astype(o_ref.dtype)
            else:
                scale = GDN_HEAD_DIM ** -0.5 if kind == "q" else 1.0
                for g in range(MXU_N // GDN_HEAD_DIM):
                    yg = y[:, g * GDN_HEAD_DIM:(g + 1) * GDN_HEAD_DIM]
                    yn = yg * lax.rsqrt(jnp.sum(yg * yg, axis=-1, keepdims=True) + L2_EPS)
                    sl = slice(cs.start + g * GDN_HEAD_DIM, cs.start + (g + 1) * GDN_HEAD_DIM)
                    o_ref[:, sl] = (yn * scale).astype(o_ref.dtype)

    pl.when(j < n_q)(functools.partial(conv_tile, "q"))
    pl.when((j >= n_q) & (j < n_qk))(functools.partial(conv_tile, "k"))
    pl.when((j >= n_qk) & (j < n_conv))(functools.partial(conv_tile, "v"))

    @pl.when(j >= n_conv)
    def _():
        for cs in groups:
            o_ref[:, cs] = jnp.dot(a, w_ref[:, cs], preferred_element_type=F32).astype(o_ref.dtype)


def _gdn_proj(a, w, conv_w, *, key_dim, seq, tm=1024, tn=1024):
    m, k = a.shape
    n = w.shape[1]
    kw, conv_dim = conv_w.shape
    tm, tn = min(tm, m, seq), min(tn, n)
    n_conv = conv_dim // tn
    kern = functools.partial(_gdn_proj_kernel, n_q=key_dim // tn, n_qk=2 * key_dim // tn, n_conv=n_conv,
                             per_seq=seq // tm, kw=kw)
    return pl.pallas_call(
        kern, grid=(m // tm, n // tn),
        in_specs=[pl.BlockSpec((tm, k), lambda i, j: (i, 0)),
                  pl.BlockSpec((k, tn), lambda i, j: (0, j)),
                  pl.BlockSpec((kw, tn), lambda i, j: (0, jnp.minimum(j, n_conv - 1)))],
        out_specs=pl.BlockSpec((tm, tn), lambda i, j: (i, j)),
        out_shape=jax.ShapeDtypeStruct((m, n), BF16),
        scratch_shapes=[pltpu.VMEM((n // tn, 8, tn), F32)],
        compiler_params=_cparams("arbitrary", "arbitrary"), name="gdn_proj",
    )(a, w, conv_w)


def _attn_kernel(lam_ref, q_ref, k_ref, v_ref, sub_ref, o_ref, m_ref, l_ref, acc_ref, alpha_ref, p_ref, *,
                 blk, rq, rc, dh, lambda_init):
    qi, ki = pl.program_id(2), pl.program_id(3)

    @pl.when(ki == 0)
    def _():
        m_ref[...] = jnp.full(m_ref.shape, NEG, F32)
        l_ref[...] = jnp.zeros(l_ref.shape, F32)
        acc_ref[...] = jnp.zeros(acc_ref.shape, F32)

    def step(n_keys, diag_key0):
        v = v_ref[:n_keys]
        n_t = n_keys // LANE
        if diag_key0 is not None:
            row = lax.broadcasted_iota(I32, (rc, LANE), 0)
            col = lax.broadcasted_iota(I32, (rc, LANE), 1)
        for q0 in range(0, blk, rq):
            for j in range(2):
                sl = slice(j * dh, (j + 1) * dh)
                s = lax.dot_general(q_ref[q0:q0 + rq, sl], k_ref[:n_keys, sl], (((1,), (1,)), ((), ())),
                                    preferred_element_type=F32)
                for r0 in range(0, rq, rc):
                    rs = slice(q0 + r0, q0 + r0 + rc)
                    tiles = [s[r0:r0 + rc, n * LANE:(n + 1) * LANE] for n in range(n_t)]
                    if diag_key0 is not None:
                        tiles = [tl if n * LANE < diag_key0 else
                                 jnp.where(col + (n * LANE - diag_key0) <= row + (q0 + r0), tl, NEG)
                                 for n, tl in enumerate(tiles)]
                    mx = tiles[0]
                    for tl in tiles[1:]:
                        mx = jnp.maximum(mx, tl)
                    m_prev = m_ref[j, rs]
                    m_new = jnp.maximum(m_prev, jnp.max(mx, axis=1, keepdims=True))
                    alpha = jnp.exp2(m_prev - m_new)
                    ps = [jnp.exp2(tl - m_new) for tl in tiles]
                    psum = ps[0]
                    for pt in ps[1:]:
                        psum = psum + pt
                    l_ref[j, rs] = alpha * l_ref[j, rs] + psum
                    m_ref[j, rs] = m_new
                    alpha_ref[j, rs] = alpha
                    p_ref[j, rs, :n_keys] = jnp.concatenate(ps, axis=1).astype(BF16)
                qs = slice(q0, q0 + rq)
                pv = jnp.dot(p_ref[j, qs, :n_keys], v, preferred_element_type=F32)
                for n in range(pv.shape[1] // LANE):
                    vl = slice(n * LANE, (n + 1) * LANE)
                    acc_ref[j, qs, vl] = alpha_ref[j, qs] * acc_ref[j, qs, vl] + pv[:, vl]

    last = qi // 2

    @pl.when(ki < last)
    def _():
        step(2 * blk, None)

    @pl.when((ki == last) & (qi % 2 == 1))
    def _():
        step(2 * blk, blk)

    @pl.when((ki == last) & (qi % 2 == 0))
    def _():
        step(blk, 0)

    @pl.when(ki == pl.num_programs(3) - 1)
    def _():
        lv = lam_ref[...]
        lam = (jnp.exp(jnp.sum(lv[0:1] * lv[1:2], axis=1, keepdims=True))
               - jnp.exp(jnp.sum(lv[2:3] * lv[3:4], axis=1, keepdims=True)) + lambda_init)
        l0 = jnp.sum(l_ref[0], axis=1, keepdims=True)
        l1 = jnp.sum(l_ref[1], axis=1, keepdims=True)
        o = acc_ref[0] / l0 - lam * (acc_ref[1] / l1)
        o = o * lax.rsqrt(jnp.mean(o * o, axis=-1, keepdims=True) + RMS_EPS) * sub_ref[...]
        o_ref[...] = (o * (1.0 - lambda_init)).astype(o_ref.dtype)


def _diff_attention(qkv, lam_vec, subln, *, batch, seq, layer_idx):
    t, d3 = qkv.shape
    d = d3 // 3
    dv = d // DA_HEADS
    dh = dv // 2
    blk = 512
    nb = seq // blk
    lambda_init = 0.8 - 0.6 * math.exp(-0.3 * layer_idx)
    kern = functools.partial(_attn_kernel, blk=blk, rq=128, rc=128, dh=dh, lambda_init=lambda_init)
    nkb = nb // 2
    kv_row = lambda b, qi, ki: b * nkb + jnp.minimum(ki, qi // 2)
    return pl.pallas_call(
        kern, grid=(batch, DA_HEADS, nb, nkb),
        in_specs=[pl.BlockSpec((4, dh), lambda b, h, qi, ki: (0, 0)),
                  pl.BlockSpec((blk, dv), lambda b, h, qi, ki: (b * nb + qi, h)),
                  pl.BlockSpec((2 * blk, dv), lambda b, h, qi, ki: (kv_row(b, qi, ki), DA_HEADS + h)),
                  pl.BlockSpec((2 * blk, dv), lambda b, h, qi, ki: (kv_row(b, qi, ki), 2 * DA_HEADS + h)),
                  pl.BlockSpec((1, dv), lambda b, h, qi, ki: (0, 0))],
        out_specs=pl.BlockSpec((blk, dv), lambda b, h, qi, ki: (b * nb + qi, h)),
        out_shape=jax.ShapeDtypeStruct((t, d), BF16),
        scratch_shapes=[pltpu.VMEM((2, blk, LANE), F32), pltpu.VMEM((2, blk, LANE), F32),
                        pltpu.VMEM((2, blk, dv), F32), pltpu.VMEM((2, blk, LANE), F32),
                        pltpu.VMEM((2, blk, 2 * blk), BF16)],
        compiler_params=_cparams("parallel", "parallel", "parallel", "arbitrary"), name="diff_attn",
    )(lam_vec, qkv, qkv, qkv, subln.reshape(1, dv))


def _split3(x):
    hi = x.astype(BF16)
    r = x - hi.astype(F32)
    mid = r.astype(BF16)
    lo = (r - mid.astype(F32)).astype(BF16)
    return hi, mid, lo


def _gates_kernel(ba_ref, alog_ref, dt_ref, beta_ref, gc_ref):
    x = ba_ref[...]
    tm = x.shape[0]
    b_raw, a_raw = x[:, :LANE], x[:, LANE:]
    beta_ref[...] = 1.0 / (1.0 + jnp.exp(-b_raw))
    z = a_raw + dt_ref[...]
    softplus = jnp.maximum(z, 0.0) + jnp.log(1.0 + jnp.exp(-jnp.abs(z)))
    g = -jnp.exp(alog_ref[...]) * softplus
    row = lax.broadcasted_iota(I32, (tm, tm), 0)
    col = lax.broadcasted_iota(I32, (tm, tm), 1)
    tri = jnp.where((row >= col) & (row // GDN_CHUNK == col // GDN_CHUNK), 1.0, 0.0).astype(BF16)
    gc_ref[...] = sum(jnp.dot(tri, part, preferred_element_type=F32) for part in _split3(g))


def _gdn_gates(ba, a_log, dt_bias):
    t = ba.shape[0]
    hv = a_log.shape[0]
    tm = 256
    pad = lambda v: jnp.pad(v.astype(F32), (0, LANE - hv)).reshape(1, LANE)
    blk = pl.BlockSpec((tm, LANE), lambda i: (i, 0))
    vec = pl.BlockSpec((1, LANE), lambda i: (0, 0))
    return pl.pallas_call(
        _gates_kernel, grid=(t // tm,),
        in_specs=[pl.BlockSpec((tm, 2 * LANE), lambda i: (i, 0)), vec, vec],
        out_specs=[blk, blk],
        out_shape=[jax.ShapeDtypeStruct((t, LANE), F32)] * 2,
        compiler_params=_cparams("parallel"), name="gdn_gates",
    )(ba, pad(a_log), pad(dt_bias))


def _delta_kernel(q_ref, k_ref, v_ref, z_ref, beta_ref, gc_ref, gct_ref, nw_ref, o_ref, state_ref, *,
                  heads, rows):
    hg, blk = pl.program_id(1), pl.program_id(2)
    dk = GDN_HEAD_DIM
    c = GDN_CHUNK
    n_chunks = rows // c

    @pl.when(blk == 0)
    def _():
        state_ref[...] = jnp.zeros(state_ref.shape, F32)

    n_pairs = rows // LANE
    ri = lax.broadcasted_iota(I32, (c, LANE), 0)
    li = lax.broadcasted_iota(I32, (c, LANE), 1)
    lj = li & (c - 1)
    left = li < c
    strict = ri > lj
    causal = ri >= lj
    same16 = (ri >> 4) == (lj >> 4)
    same32 = (ri >> 5) == (lj >> 5)
    m16 = strict & same16
    m32 = strict & same32 & jnp.logical_not(same16)
    m64 = strict & jnp.logical_not(same32)
    eye = jnp.where(ri == lj, 1.0, 0.0)
    bd_mask = ((lax.broadcasted_iota(I32, (LANE, LANE), 0) < c)
               == (lax.broadcasted_iota(I32, (LANE, LANE), 1) < c))
    lane = lax.broadcasted_iota(I32, (rows, LANE), 1)
    contract_last = (((1,), (1,)), ((), ()))
    contract_first = (((0,), (0,)), ((), ()))

    def bd(p):
        return jnp.where(bd_mask, jnp.concatenate([p, p], axis=0), 0.0).astype(BF16)

    def pm(p, q):
        return jnp.dot(p.astype(BF16), bd(q), preferred_element_type=F32)

    def pack(full):
        return jnp.where(left, full[:c], full[c:])

    gc_all, beta_all = gc_ref[...], beta_ref[...]
    eg_all = jnp.exp(gc_all)
    pairs = [(g, p) for g in range(heads) for p in range(n_pairs)]
    prow = lambda x, p: x[p * LANE:(p + 1) * LANE]

    kkt, qkt = {}, {}
    for kh in range(heads // 2):
        ks = slice(kh * dk, (kh + 1) * dk)
        for p in range(n_pairs):
            kp = k_ref[p * LANE:(p + 1) * LANE, ks]
            qp = q_ref[p * LANE:(p + 1) * LANE, ks]
            kkt[kh, p] = pack(lax.dot_general(kp, kp, contract_last, preferred_element_type=F32))
            qkt[kh, p] = pack(lax.dot_general(qp, kp, contract_last, preferred_element_type=F32))

    hd = []
    for g in range(heads):
        pick = lane == hg * heads + g
        col = lambda x: jnp.sum(jnp.where(pick, x, 0.0), axis=1, keepdims=True)
        gc_col, beta_col, eg_col = col(gc_all), col(beta_all), col(eg_all)
        ks = slice((g // 2) * dk, (g // 2 + 1) * dk)
        kf = k_ref[:, ks].astype(F32)
        k_beta = kf * beta_col
        y = jnp.concatenate([v_ref[:, g * dk:(g + 1) * dk].astype(F32) * beta_col, k_beta * eg_col],
                            axis=1).astype(BF16)
        gl = [gc_col[ci * c + c - 1:ci * c + c] for ci in range(n_chunks)]
        gl_col = jnp.concatenate([jnp.broadcast_to(v, (c, 1)) for v in gl], axis=0)
        hd.append(dict(gc=gc_col, beta=beta_col, y=y,
                       q_dec=(q_ref[:, ks].astype(F32) * eg_col).astype(BF16),
                       k_dec=(kf * jnp.exp(gl_col - gc_col)).astype(BF16),
                       g_end=[jnp.exp(v) for v in gl]))

    a, intra = {}, {}
    for g, p in pairs:
        gcp = jnp.where(left, prow(hd[g]["gc"], p)[:c], prow(hd[g]["gc"], p)[c:])
        bp = jnp.where(left, prow(hd[g]["beta"], p)[:c], prow(hd[g]["beta"], p)[c:])
        decay = jnp.exp(jnp.where(causal, gcp - gct_ref[g][:, p * LANE:(p + 1) * LANE], NEG))
        a[g, p] = kkt[g // 2, p] * bp * decay
        intra[g, p] = qkt[g // 2, p] * decay

    d1 = {n: jnp.where(m16, a[n], 0.0) for n in pairs}
    d2 = {n: pm(d1[n], d1[n]) for n in pairs}
    d4 = {n: pm(d2[n], d2[n]) for n in pairs}
    x = {n: eye - d1[n] for n in pairs}
    x = {n: x[n] + pm(x[n], d2[n]) for n in pairs}
    d8 = {n: pm(d4[n], d4[n]) for n in pairs}
    x = {n: x[n] + pm(x[n], d4[n]) for n in pairs}
    x = {n: x[n] + pm(x[n], d8[n]) for n in pairs}
    for mask in (m32, m64):
        t = {n: pm(jnp.where(mask, a[n], 0.0), x[n]) for n in pairs}
        x = {n: x[n] - pm(x[n], t[n]) for n in pairs}
    uw = {(g, p): jnp.dot(bd(x[g, p]), prow(hd[g]["y"], p), preferred_element_type=F32) for g, p in pairs}

    s = [state_ref[g] for g in range(heads)]
    v_new = {}
    o_state = {}
    for ci in range(n_chunks):
        p, half = divmod(ci, 2)
        hs = slice(half * c, (half + 1) * c)
        rs = slice(ci * c, (ci + 1) * c)
        sb = [s[g].astype(BF16) for g in range(heads)]
        for g in range(heads):
            u_c, w_c = uw[g, p][hs, :dk], uw[g, p][hs, dk:]
            v_new[g, ci] = u_c - jnp.dot(w_c.astype(BF16), sb[g], preferred_element_type=F32)
            o_state[g, ci] = jnp.dot(hd[g]["q_dec"][rs], sb[g], preferred_element_type=F32)
        for g in range(heads):
            s[g] = s[g] * hd[g]["g_end"][ci] + lax.dot_general(
                hd[g]["k_dec"][rs], v_new[g, ci].astype(BF16), contract_first, preferred_element_type=F32)
    for g in range(heads):
        state_ref[g] = s[g]

    for g in range(heads):
        o_pairs = []
        for p in range(n_pairs):
            vn = jnp.concatenate([v_new[g, 2 * p], v_new[g, 2 * p + 1]], axis=0).astype(BF16)
            o_in = jnp.dot(bd(intra[g, p]), vn, preferred_element_type=F32)
            o_pairs.append(o_in + jnp.concatenate([o_state[g, 2 * p], o_state[g, 2 * p + 1]], axis=0))
        o = jnp.concatenate(o_pairs, axis=0)
        o = o * lax.rsqrt(jnp.mean(o * o, axis=-1, keepdims=True) + RMS_EPS) * nw_ref[...]
        vs = slice(g * dk, (g + 1) * dk)
        o_ref[:, vs] = (o * _silu(z_ref[:, vs].astype(F32))).astype(o_ref.dtype)


def _delta_rule(proj, key_dim, val_dim, beta, gc, gct, norm_w, *, batch, seq):
    t = proj.shape[0]
    vdim = val_dim
    dk = GDN_HEAD_DIM
    hv = vdim // dk
    heads = 8
    rows = 256
    nblk = seq // rows
    qw, vw = heads // 2 * dk, heads * dk
    kb, vb, zb = key_dim // qw, 2 * key_dim // vw, (2 * key_dim + val_dim) // vw
    kern = functools.partial(_delta_kernel, heads=heads, rows=rows)
    tok = lambda b, hg, i: (b * nblk + i, 0)
    return pl.pallas_call(
        kern, grid=(batch, hv // heads, nblk),
        in_specs=[pl.BlockSpec((rows, qw), lambda b, hg, i: (b * nblk + i, hg)),
                  pl.BlockSpec((rows, qw), lambda b, hg, i: (b * nblk + i, kb + hg)),
                  pl.BlockSpec((rows, vw), lambda b, hg, i: (b * nblk + i, vb + hg)),
                  pl.BlockSpec((rows, vw), lambda b, hg, i: (b * nblk + i, zb + hg)),
                  pl.BlockSpec((rows, LANE), tok),
                  pl.BlockSpec((rows, LANE), tok),
                  pl.BlockSpec((heads, 1, rows), lambda b, hg, i: (hg, 0, b * nblk + i)),
                  pl.BlockSpec((1, dk), lambda b, hg, i: (0, 0))],
        out_specs=pl.BlockSpec((rows, vw), lambda b, hg, i: (b * nblk + i, hg)),
        out_shape=jax.ShapeDtypeStruct((t, vdim), BF16),
        scratch_shapes=[pltpu.VMEM((heads, dk, dk), F32)],
        compiler_params=_cparams("parallel", "parallel", "arbitrary"), name="delta_rule",
    )(proj, proj, proj, proj, beta, gc, gct, norm_w.reshape(1, dk))


def _ffn_kernel(te_ref, *refs, mode):
    del te_ref
    if mode == "resid":
        x_ref, wg_ref, wu_ref, wd_ref, h_ref, g_ref, o_ref, acc_ref = refs
    else:
        x_ref, wg_ref, wu_ref, wd_ref, o_ref, acc_ref = refs
    f = pl.program_id(1)

    @pl.when(f == 0)
    def _():
        acc_ref[...] = jnp.zeros(acc_ref.shape, F32)

    x = x_ref[...]
    tf = wg_ref.shape[2]
    part = None
    for n in range(tf // MXU_N):
        cs = slice(n * MXU_N, (n + 1) * MXU_N)
        gate = jnp.dot(x, wg_ref[0, :, cs], preferred_element_type=F32)
        up = jnp.dot(x, wu_ref[0, :, cs], preferred_element_type=F32)
        dn = jnp.dot((_silu(gate) * up).astype(BF16), wd_ref[0, cs, :], preferred_element_type=F32)
        part = dn if part is None else part + dn
    acc_ref[...] += part

    @pl.when(f == pl.num_programs(1) - 1)
    def _():
        if mode == "resid":
            o_ref[...] = h_ref[...] + g_ref[0] * acc_ref[...]
        else:
            o_ref[...] = acc_ref[...].astype(o_ref.dtype)


def _ffn(x, wg, wu, wd, tile_expert, *, tm, mode, seq=None, h=None, gate=None):
    m, d = x.shape
    ff = wg.shape[2]
    tf = 512
    xs = pl.BlockSpec((tm, d), lambda i, f, te: (i, 0))
    in_specs = [xs,
                pl.BlockSpec((1, d, tf), lambda i, f, te: (te[i], 0, f)),
                pl.BlockSpec((1, d, tf), lambda i, f, te: (te[i], 0, f)),
                pl.BlockSpec((1, tf, d), lambda i, f, te: (te[i], f, 0))]
    args = [x, wg, wu, wd]
    if mode == "resid":
        per_seq = seq // tm
        in_specs += [xs, pl.BlockSpec((1, 1, d), lambda i, f, te: (i // per_seq, 0, 0))]
        args += [h, gate]
        out_dtype = F32
    else:
        out_dtype = BF16
    return pl.pallas_call(
        functools.partial(_ffn_kernel, mode=mode),
        grid_spec=pltpu.PrefetchScalarGridSpec(
            num_scalar_prefetch=1, grid=(m // tm, ff // tf), in_specs=in_specs, out_specs=xs,
            scratch_shapes=[pltpu.VMEM((tm, d), F32)]),
        out_shape=jax.ShapeDtypeStruct((m, d), out_dtype),
        compiler_params=_cparams("parallel", "arbitrary"), name="swiglu_" + mode,
    )(tile_expert, *args)


def _scatter_kernel(sa_ref, sb_ref, x_ref, init_ref, o_ref, sem, *, tm):
    del init_ref
    def issue(r, carry):
        pltpu.make_async_copy(x_ref.at[r], o_ref.at[sa_ref[r]], sem).start()
        pltpu.make_async_copy(x_ref.at[r], o_ref.at[sb_ref[r]], sem).start()
        return carry
    lax.fori_loop(0, tm, issue, 0)
    drain = pltpu.make_async_copy(x_ref, o_ref.at[pl.ds(0, tm)], sem)
    drain.wait()
    drain.wait()


def _scatter_rows(x, slot_a, slot_b, n_slots, *, tm):
    t, s, l = x.shape
    ispec = pl.BlockSpec((tm,), lambda i: (i,), memory_space=pltpu.SMEM)
    return pl.pallas_call(
        functools.partial(_scatter_kernel, tm=tm), grid=(t // tm,),
        in_specs=[ispec, ispec, pl.BlockSpec((tm, s, l), lambda i: (i, 0, 0)), pl.BlockSpec(memory_space=pl.ANY)],
        out_specs=pl.BlockSpec(memory_space=pl.ANY),
        out_shape=jax.ShapeDtypeStruct((n_slots, s, l), x.dtype),
        scratch_shapes=[pltpu.SemaphoreType.DMA(())],
        input_output_aliases={3: 0},
        compiler_params=_cparams("arbitrary"), name="scatter_rows",
    )(slot_a, slot_b, x, jnp.zeros((n_slots, s, l), x.dtype))


def _combine_kernel(ia_ref, ib_ref, wa_ref, wb_ref, src_ref, o_ref, a_buf, b_buf, sem_a, sem_b, *, tm):
    def issue(r, carry):
        pltpu.make_async_copy(src_ref.at[ia_ref[r]], a_buf.at[r], sem_a).start()
        pltpu.make_async_copy(src_ref.at[ib_ref[r]], b_buf.at[r], sem_b).start()
        return carry
    lax.fori_loop(0, tm, issue, 0)
    pltpu.make_async_copy(src_ref.at[pl.ds(0, tm)], a_buf, sem_a).wait()
    pltpu.make_async_copy(src_ref.at[pl.ds(0, tm)], b_buf, sem_b).wait()
    o_ref[...] = wa_ref[...] * a_buf[...].astype(F32) + wb_ref[...] * b_buf[...].astype(F32)


def _combine_rows(src, idx_a, idx_b, w_a, w_b, *, tm):
    _, s, l = src.shape
    t = idx_a.shape[0]
    ispec = pl.BlockSpec((tm,), lambda i: (i,), memory_space=pltpu.SMEM)
    wspec = pl.BlockSpec((tm, 1, l), lambda i: (i, 0, 0))
    return pl.pallas_call(
        functools.partial(_combine_kernel, tm=tm), grid=(t // tm,),
        in_specs=[ispec, ispec, wspec, wspec, pl.BlockSpec(memory_space=pl.ANY)],
        out_specs=pl.BlockSpec((tm, s, l), lambda i: (i, 0, 0)),
        out_shape=jax.ShapeDtypeStruct((t, s, l), F32),
        scratch_shapes=[pltpu.VMEM((tm, s, l), src.dtype), pltpu.VMEM((tm, s, l), src.dtype),
                        pltpu.SemaphoreType.DMA(()), pltpu.SemaphoreType.DMA(())],
        compiler_params=_cparams("arbitrary"), name="combine_rows",
    )(idx_a, idx_b, w_a, w_b, src)


def _dispatch_plan(sel, gates, n_experts, tm):
    t = sel.shape[0]
    chosen = sel[:, :n_experts] > 0.0
    seli = chosen.astype(I32)
    counts = jnp.sum(seli, axis=0)
    padded = (counts + tm - 1) // tm * tm
    ends = jnp.cumsum(padded)
    starts = ends - padded
    rank = jnp.cumsum(seli, axis=0) - seli
    slot = starts[None, :] + rank
    n_slots = t * TOP_K + n_experts * tm
    slot_a = jnp.min(jnp.where(chosen, slot, n_slots), axis=1).astype(I32)
    slot_b = jnp.max(jnp.where(chosen, slot, -1), axis=1).astype(I32)
    g = gates[:, :n_experts]
    lanes = lambda w: jnp.broadcast_to(w[:, None, None], (t, 1, LANE))
    w_a = lanes(jnp.sum(jnp.where(chosen & (slot == slot_a[:, None]), g, 0.0), axis=1))
    w_b = lanes(jnp.sum(jnp.where(chosen & (slot == slot_b[:, None]), g, 0.0), axis=1))
    tile_start = jnp.arange(n_slots // tm, dtype=I32) * tm
    tile_expert = jnp.minimum(jnp.searchsorted(ends, tile_start, side="right"), n_experts - 1).astype(I32)
    return slot_a, slot_b, w_a, w_b, tile_expert, n_slots


def _rope_tables(positions, rot_dim):
    half = rot_dim // 2
    inv_freq = 1.0 / (ROPE_THETA ** (jnp.arange(0, rot_dim, 2, dtype=F32) / rot_dim))
    ang = positions.astype(F32).reshape(-1, 1) * inv_freq
    cos, sin = jnp.cos(ang), jnp.sin(ang)
    t = ang.shape[0]
    ones = jnp.ones((t, LANE - rot_dim), F32)
    zeros = jnp.zeros((t, LANE - half), F32)
    c = jnp.concatenate([cos, cos, ones], axis=1)
    s1 = jnp.concatenate([-sin, zeros], axis=1)
    s2 = jnp.concatenate([zeros[:, :half], sin, zeros[:, :LANE - rot_dim]], axis=1)
    return c, s1, s2


def kernel(x, c, positions, ada_w, ada_b, norm1_w, norm2_w, attn_w_in, attn_lambda, attn_subln, attn_w_out, gdn_w_in, gdn_conv_w, gdn_a_log, gdn_dt_bias, gdn_norm_w, gdn_w_out, ffn_w_gate, ffn_w_up, ffn_w_down, moe_router, moe_w_gate, moe_w_up, moe_w_down, final_norm_w):
    batch, seq, d = x.shape
    t = batch * seq
    depth = ada_w.shape[0]
    n_experts = moe_router.shape[-1]
    dh = d // DA_HEADS // 2
    key_dim = d
    val_dim = gdn_w_out.shape[1]
    conv_dim = gdn_conv_w.shape[-1]
    hv = gdn_a_log.shape[-1]
    moe_tm = 512
    s_sub = d // LANE

    mod = _adaln(c, ada_w, ada_b)
    rope = _rope_tables(positions, dh // 4)
    h = x.reshape(t, d)
    delta = gate = None
    ffn_w = [w.astype(BF16) for w in (ffn_w_gate, ffn_w_up, ffn_w_down)]
    moe_w = [w.astype(BF16).reshape((-1,) + w.shape[2:]) for w in (moe_w_gate, moe_w_up, moe_w_down)]

    for i in range(depth):
        j = i // 2
        sh1, sc1, g1, sh2, sc2, g2 = [mod[i, :, n * d:(n + 1) * d].reshape(batch, 1, d) for n in range(6)]
        if delta is None:
            u = _norm(h, norm1_w[i], seq=seq, sc=sc1, sh=sh1)[0]
        else:
            h, u = _norm(h, norm1_w[i], seq=seq, delta=delta, gate=gate, sc=sc1, sh=sh1)
            delta = gate = None
        if i % 2 == 0:
            qkv = _matmul(u, attn_w_in[j].astype(BF16), mode="rope", rope=rope, rope_cols=(d, 2 * d),
                          rope_half=dh // 8, q_scale=dh ** -0.5 * math.log2(math.e), tn=1024)
            o = _diff_attention(qkv, attn_lambda[j], attn_subln[j], batch=batch, seq=seq, layer_idx=i)
            h = _matmul(o, attn_w_out[j].astype(BF16), mode="resid", seq=seq, h=h, gate=g1, out_dtype=F32)
        else:
            w_in = gdn_w_in[j]
            n_main = conv_dim + val_dim
            proj = _gdn_proj(u, w_in[:, :n_main].astype(BF16), gdn_conv_w[j], key_dim=key_dim, seq=seq)
            w_ba = jnp.zeros((d, 2 * LANE), F32)
            w_ba = w_ba.at[:, :hv].set(w_in[:, n_main:n_main + hv]).at[:, LANE:LANE + hv].set(w_in[:, n_main + hv:])
            ba = _matmul(u, w_ba.astype(BF16), out_dtype=F32, tn=2 * LANE)
            beta, gc = _gdn_gates(ba, gdn_a_log[j], gdn_dt_bias[j])
            gct = gc[:, :hv].T.reshape(hv, 1, t)
            o = _delta_rule(proj, key_dim, val_dim, beta, gc, gct, gdn_norm_w[j], batch=batch, seq=seq)
            h = _matmul(o, gdn_w_out[j].astype(BF16), mode="resid", seq=seq, h=h, gate=g1, out_dtype=F32)
        if i % 2 == 0:
            u = _norm(h, norm2_w[i], seq=seq, sc=sc2, sh=sh2)[0]
            h = _ffn(u, *ffn_w, jnp.full((t // moe_tm,), j, I32), tm=moe_tm, mode="resid", seq=seq, h=h, gate=g2)
        else:
            router = jnp.pad(moe_router[j], ((0, 0), (0, LANE - n_experts)))
            u, gates, sel = _norm(h, norm2_w[i], seq=seq, sc=sc2, sh=sh2, router=router, n_experts=n_experts)
            slot_a, slot_b, w_a, w_b, tile_expert, n_slots = _dispatch_plan(sel, gates, n_experts, moe_tm)
            xs = _scatter_rows(u.reshape(t, s_sub, LANE), slot_a, slot_b, n_slots, tm=moe_tm)
            ys = _ffn(xs.reshape(-1, d), *moe_w, tile_expert + j * n_experts, tm=moe_tm, mode="plain")
            delta = _combine_rows(ys.reshape(-1, s_sub, LANE), slot_a, slot_b, w_a, w_b, tm=moe_tm).reshape(t, d)
            gate = g2
    if delta is None:
        out = _norm(h, final_norm_w, seq=seq, out_dtype=F32)[0]
    else:
        out = _norm(h, final_norm_w, seq=seq, delta=delta, gate=gate, out_dtype=F32)[1]
    return out.reshape(batch, seq, d)
```
